```python
import math
import jax, jax.numpy as jnp
from jax import lax
import numpy as np

D_MODEL = 2048
BATCH = 8
SEQ = 4096
DEPTH = 2

N_BRANCH = 4
BRANCH_W = D_MODEL // N_BRANCH
HEAD_DIM = 128
N_HEADS = BRANCH_W // HEAD_DIM
SG_CHUNK = 128
SG_GROUPS = N_HEADS
SG_GDIM = BRANCH_W // SG_GROUPS
CONV_W = 3
DIL_CFG = ((128, 1), (512, 4), (2048, 16))
N_DIL = len(DIL_CFG)
DIL_QBLOCK = 128
DIFF_DIM = HEAD_DIM // 2
DIFF_QBLOCK = 128
ROPE_THETA = 10000.0
W_A = 2 * BRANCH_W
W_B = 3 * BRANCH_W
W_C = 3 * N_DIL * BRANCH_W
W_D = 3 * BRANCH_W
W_IN = W_A + W_B + W_C + W_D
N_GROUPS = 4
EXP_PER_GROUP = 8
N_EXPERTS = N_GROUPS * EXP_PER_GROUP
TOP_K = 2
D_EXPERT = D_MODEL // 4
EPS = 1e-6

kernel_name = 'hybrid_gated_parallel_mixers_hmoe'


def rmsnorm(x, g):
    xf = x.astype(jnp.float32)
    y = xf * lax.rsqrt(jnp.mean(xf * xf, axis=-1, keepdims=True) + EPS)
    return (y * g.astype(jnp.float32)).astype(x.dtype)


def rope(x):
    s, hd = x.shape[1], x.shape[-1]
    half = hd // 2
    inv = ROPE_THETA ** (-jnp.arange(half, dtype=jnp.float32) / half)
    ang = jnp.arange(s, dtype=jnp.float32)[:, None] * inv[None, :]
    shape = (s,) + (1,) * (x.ndim - 3) + (half,)
    cos = jnp.cos(ang).reshape(shape)
    sin = jnp.sin(ang).reshape(shape)
    xf = x.astype(jnp.float32)
    x1, x2 = xf[..., :half], xf[..., half:]
    return jnp.concatenate([x1 * cos - x2 * sin, x2 * cos + x1 * sin], axis=-1).astype(x.dtype)


def spatial_gating(za, norm_g, w_s, b_s):
    b, s, _ = za.shape
    u, v = jnp.split(za, 2, axis=-1)
    v = rmsnorm(v, norm_g).reshape(b, s // SG_CHUNK, SG_CHUNK, SG_GROUPS, SG_GDIM)
    pos = jnp.arange(SG_CHUNK)
    causal = pos[:, None] >= pos[None, :]
    w = jnp.where(causal[None], w_s, 0.0)
    mixed = jnp.einsum('gij,bnjgc->bnigc', w, v) + b_s.T[None, None, :, :, None]
    return u * mixed.reshape(b, s, BRANCH_W)


def short_conv(zb, conv_w):
    bg, cg, hb = jnp.split(zb, 3, axis=-1)
    z = cg * hb
    y = lax.conv_general_dilated(
        z, conv_w[:, None, :].astype(z.dtype), window_strides=(1,),
        padding=[(CONV_W - 1, 0)], dimension_numbers=('NWC', 'WIO', 'NWC'),
        feature_group_count=BRANCH_W)
    return bg * y


def banded_dilated_attention(q, k, v, dilation, n_back):
    b, s, h, hd = q.shape
    L = s // dilation
    qb = min(DIL_QBLOCK, L)
    nblk = -(-L // qb)
    Lp = nblk * qb
    kb = qb + n_back

    def to_sub(t):
        return t.reshape(b, L, dilation, h, hd).transpose(0, 2, 3, 1, 4)

    qs = jnp.pad(to_sub(q), ((0, 0), (0, 0), (0, 0), (0, Lp - L), (0, 0)))
    qs = qs.reshape(b, dilation, h, nblk, qb, hd)
    pad_kv = ((0, 0), (0, 0), (0, 0), (n_back, Lp - L), (0, 0))
    ks = jnp.pad(to_sub(k), pad_kv)
    vs = jnp.pad(to_sub(v), pad_kv)
    kidx = jnp.arange(nblk)[:, None] * qb + jnp.arange(kb)[None, :]
    kg = ks[:, :, :, kidx]
    vg = vs[:, :, :, kidx]
    scores = jnp.einsum('brhnqd,brhnkd->brhnqk', qs, kg).astype(jnp.float32) * (hd ** -0.5)
    qpos = jnp.arange(nblk)[:, None] * qb + jnp.arange(qb)[None, :]
    kpos = kidx - n_back
    dist = qpos[:, :, None] - kpos[:, None, :]
    band = (dist >= 0) & (dist <= n_back) & (kpos[:, None, :] >= 0)
    scores = jnp.where(band, scores, -jnp.inf)
    mx = jnp.max(scores, axis=-1, keepdims=True)
    p = jnp.exp(scores - mx)
    denom = jnp.sum(p, axis=-1, keepdims=True)
    out = jnp.einsum('brhnqk,brhnkd->brhnqd', (p / denom).astype(v.dtype), vg)
    lse = (mx + jnp.log(denom))[..., 0]
    out = out.reshape(b, dilation, h, Lp, hd)[:, :, :, :L]
    out = out.transpose(0, 3, 1, 2, 4).reshape(b, s, h, hd)
    lse = lse.reshape(b, dilation, h, Lp)[..., :L].transpose(0, 3, 1, 2).reshape(b, s, h)
    return out, lse


def dilated_mixture(zc, qn, kn):
    b, s, _ = zc.shape
    zc = zc.reshape(b, s, 3, N_DIL, N_HEADS, HEAD_DIM)
    q = rope(rmsnorm(zc[:, :, 0], qn))
    k = rope(rmsnorm(zc[:, :, 1], kn))
    v = zc[:, :, 2]
    outs, lses = [], []
    for g, (window, dil) in enumerate(DIL_CFG):
        o, l = banded_dilated_attention(q[:, :, g], k[:, :, g], v[:, :, g], dil, window // dil)
        outs.append(o)
        lses.append(l)
    alpha = jax.nn.softmax(jnp.stack(lses, axis=-1), axis=-1)
    o = jnp.einsum('bshg,bshgd->bshd', alpha.astype(v.dtype), jnp.stack(outs, axis=-2))
    return o.reshape(b, s, BRANCH_W)


def diff_attention(zd, qn, kn, lq1, lk1, lq2, lk2, sub_g, lam_init):
    b, s, _ = zd.shape
    zq, zk, v = jnp.split(zd, 3, axis=-1)
    q = rope(rmsnorm(zq.reshape(b, s, N_HEADS, 2, DIFF_DIM), qn))
    k = rope(rmsnorm(zk.reshape(b, s, N_HEADS, 2, DIFF_DIM), kn))
    v = v.reshape(b, s, N_HEADS, 2 * DIFF_DIM)
    f32 = jnp.float32
    lam = (jnp.exp(jnp.sum(lq1.astype(f32) * lk1.astype(f32)))
           - jnp.exp(jnp.sum(lq2.astype(f32) * lk2.astype(f32))) + lam_init)
    qb = min(DIFF_QBLOCK, s)
    nblk = s // qb
    qblocks = q.reshape(b, nblk, qb, N_HEADS, 2, DIFF_DIM).transpose(1, 0, 2, 3, 4, 5)
    kpos = jnp.arange(s)
    scale = DIFF_DIM ** -0.5

    def block(args):
        qblk, n = args
        sc = jnp.einsum('bqhmd,bkhmd->bhmqk', qblk, k).astype(f32) * scale
        qpos = n * qb + jnp.arange(qb)
        sc = jnp.where(kpos[None, :] <= qpos[:, None], sc, -jnp.inf)
        p = jax.nn.softmax(sc, axis=-1)
        a = p[:, :, 0] - lam * p[:, :, 1]
        return jnp.einsum('bhqk,bkhd->bqhd', a.astype(v.dtype), v)

    out = lax.map(block, (qblocks, jnp.arange(nblk)))
    out = out.transpose(1, 0, 2, 3, 4).reshape(b, s, N_HEADS, 2 * DIFF_DIM)
    out = rmsnorm(out, sub_g) * (1.0 - lam_init)
    return out.reshape(b, s, BRANCH_W)


def hier_moe(h, w_rg, b_rg, w_re, b_re, w_up, w_down):
    b, s, d = h.shape
    t = h.reshape(b * s, d)
    g_prob = jax.nn.softmax((t @ w_rg + b_rg).astype(jnp.float32), axis=-1)
    g_p, g_idx = lax.top_k(g_prob, 1)
    e_logits = (t @ w_re + b_re).astype(jnp.float32).reshape(-1, N_GROUPS, EXP_PER_GROUP)
    in_group = jnp.take_along_axis(e_logits, g_idx[:, :, None], axis=1)[:, 0]
    top_v, top_i = lax.top_k(in_group, TOP_K)
    w = g_p * jax.nn.softmax(top_v, axis=-1)
    eid = g_idx * EXP_PER_GROUP + top_i
    gate = jnp.sum(jax.nn.one_hot(eid, N_EXPERTS, dtype=jnp.float32) * w[..., None], axis=1)
    gate = gate.astype(t.dtype)
    y = jnp.zeros_like(t)
    for e in range(N_EXPERTS):
        a, c = jnp.split(t @ w_up[e], 2, axis=-1)
        y = y + gate[:, e:e + 1] * ((jax.nn.silu(a) * c) @ w_down[e])
    return y.reshape(b, s, d)


def setup_inputs(seed: int = 0) -> dict:
    key = jax.random.key(seed)
    ks = jax.random.split(key, 32)
    f32 = jnp.float32
    L = DEPTH

    def nrm(k, shape, scale):
        return jax.random.normal(k, shape, f32) * scale

    def gain(k, shape):
        return 1.0 + 0.05 * jax.random.normal(k, shape, f32)

    return {
        'x': nrm(ks[0], (BATCH, SEQ, D_MODEL), 1.0),
        'norm_mix': gain(ks[1], (L, D_MODEL)),
        'w_in': nrm(ks[2], (L, D_MODEL, W_IN), D_MODEL ** -0.5),
        'sg_norm': gain(ks[3], (L, BRANCH_W)),
        'sg_w': nrm(ks[4], (L, SG_GROUPS, SG_CHUNK, SG_CHUNK), SG_CHUNK ** -0.5),
        'sg_b': gain(ks[5], (L, SG_GROUPS, SG_CHUNK)),
        'conv_w': nrm(ks[6], (L, CONV_W, BRANCH_W), CONV_W ** -0.5),
        'qn_c': gain(ks[7], (L, HEAD_DIM)),
        'kn_c': gain(ks[8], (L, HEAD_DIM)),
        'qn_d': gain(ks[9], (L, DIFF_DIM)),
        'kn_d': gain(ks[10], (L, DIFF_DIM)),
        'lam_q1': nrm(ks[11], (L, DIFF_DIM), 0.1),
        'lam_k1': nrm(ks[12], (L, DIFF_DIM), 0.1),
        'lam_q2': nrm(ks[13], (L, DIFF_DIM), 0.1),
        'lam_k2': nrm(ks[14], (L, DIFF_DIM), 0.1),
        'subln_d': gain(ks[15], (L, 2 * DIFF_DIM)),
        'w_gate': nrm(ks[16], (L, N_BRANCH, D_MODEL, D_MODEL), D_MODEL ** -0.5),
        'b_gate': nrm(ks[17], (L, N_BRANCH, D_MODEL), 0.02),
        'w_branch': nrm(ks[18], (L, N_BRANCH, BRANCH_W, D_MODEL), BRANCH_W ** -0.5),
        'w_out': nrm(ks[19], (L, D_MODEL, D_MODEL), D_MODEL ** -0.5),
        'norm_ffn': gain(ks[20], (L, D_MODEL)),
        'w_rg': nrm(ks[21], (L, D_MODEL, N_GROUPS), D_MODEL ** -0.5),
        'b_rg': nrm(ks[22], (L, N_GROUPS), 0.01),
        'w_re': nrm(ks[23], (L, D_MODEL, N_EXPERTS), D_MODEL ** -0.5),
        'b_re': nrm(ks[24], (L, N_EXPERTS), 0.01),
        'w_up': nrm(ks[25], (L, N_EXPERTS, D_MODEL, 2 * D_EXPERT), D_MODEL ** -0.5),
        'w_down': nrm(ks[26], (L, N_EXPERTS, D_EXPERT, D_MODEL), D_EXPERT ** -0.5),
    }


def reference(x, norm_mix, w_in, sg_norm, sg_w, sg_b, conv_w, qn_c, kn_c, qn_d, kn_d,
              lam_q1, lam_k1, lam_q2, lam_k2, subln_d, w_gate, b_gate, w_branch, w_out,
              norm_ffn, w_rg, b_rg, w_re, b_re, w_up, w_down):
    for l in range(DEPTH):
        lam_init = 0.8 - 0.6 * math.exp(-0.3 * l)
        h = rmsnorm(x, norm_mix[l])
        z = h @ w_in[l]
        za, zb, zc, zd = jnp.split(z, [W_A, W_A + W_B, W_A + W_B + W_C], axis=-1)
        y_a = spatial_gating(jax.nn.gelu(za), sg_norm[l], sg_w[l], sg_b[l])
        y_b = short_conv(zb, conv_w[l])
        y_c = dilated_mixture(zc, qn_c[l], kn_c[l])
        y_d = diff_attention(zd, qn_d[l], kn_d[l], lam_q1[l], lam_k1[l], lam_q2[l],
                             lam_k2[l], subln_d[l], lam_init)
        merged = jnp.zeros_like(x)
        for i, y in enumerate((y_a, y_b, y_c, y_d)):
            g = jax.nn.sigmoid(h @ w_gate[l, i] + b_gate[l, i])
            merged = merged + g * (y @ w_branch[l, i])
        x = x + merged @ w_out[l]
        x = x + hier_moe(rmsnorm(x, norm_ffn[l]), w_rg[l], b_rg[l], w_re[l], b_re[l],
                         w_up[l], w_down[l])
    return x
```

```python
import functools
import math

import jax
import jax.numpy as jnp
from jax import lax
from jax.experimental import pallas as pl
from jax.experimental.pallas import tpu as pltpu

F32 = jnp.float32
BF16 = jnp.bfloat16
I32 = jnp.int32
U32 = jnp.uint32

D_MODEL = 2048
BRANCH_W = 512
HEAD_DIM = 128
N_HEADS = 4
SG_CHUNK = 128
DIL_CFG = ((128, 1), (512, 4), (2048, 16))
N_BACK = 128
DIFF_DIM = 64
ROPE_THETA = 10000.0
W_A = 2 * BRANCH_W
W_B = 3 * BRANCH_W
W_C = 9 * BRANCH_W
N_GROUPS = 4
EXP_PER_GROUP = 8
N_EXPERTS = 32
D_EXPERT = 512
EPS = 1e-6
NEG = -1e30

LANES = 128
PACK_ROWS = D_MODEL // 2 // LANES
VMEM_LIMIT = 56 * 1024 * 1024
ROW_TILE = 256


def _cparams(sem, **kw):
    return pltpu.CompilerParams(dimension_semantics=sem, vmem_limit_bytes=VMEM_LIMIT, **kw)


def _sds(shape, dtype):
    return jax.ShapeDtypeStruct(shape, dtype)


def _proj_body(x_ref, g_ref, w_ref, z_ref, h_ref):
    @pl.when(pl.program_id(1) == 0)
    def _():
        x = x_ref[...]
        ms = jnp.mean(x * x, axis=-1, keepdims=True)
        h_ref[...] = (x * lax.rsqrt(ms + EPS) * g_ref[...]).astype(BF16)

    z_ref[...] = jnp.dot(h_ref[...], w_ref[...], preferred_element_type=F32).astype(z_ref.dtype)


def proj(x, g, w, tm=1024, tn=512):
    t, d = x.shape
    n = w.shape[1]
    return pl.pallas_call(
        _proj_body,
        grid=(t // tm, n // tn),
        in_specs=[pl.BlockSpec((tm, d), lambda i, j: (i, 0)),
                  pl.BlockSpec((1, d), lambda i, j: (0, 0)),
                  pl.BlockSpec((d, tn), lambda i, j: (0, j))],
        out_specs=[pl.BlockSpec((tm, tn), lambda i, j: (i, j)),
                   pl.BlockSpec((tm, d), lambda i, j: (i, 0))],
        out_shape=[_sds((t, n), BF16), _sds((t, d), BF16)],
        compiler_params=_cparams(("arbitrary", "arbitrary")),
        name="proj",
    )(x, g.reshape(1, d), w)


def _gelu_tanh(x):
    c = math.sqrt(2.0 / math.pi)
    return 0.5 * x * (1.0 + jnp.tanh(c * (x + 0.044715 * (x * x * x))))


def _mixa_body(z_ref, g_ref, w_ref, b_ref, o_ref):
    tm = z_ref.shape[0]
    ga = _gelu_tanh(z_ref[...].astype(F32))
    u = ga[:, :BRANCH_W]
    v = ga[:, BRANCH_W:]
    ms = jnp.mean(v * v, axis=-1, keepdims=True)
    vn = (v * lax.rsqrt(ms + EPS) * g_ref[...]).astype(BF16)
    row = lax.broadcasted_iota(I32, (SG_CHUNK, SG_CHUNK), 0)
    col = lax.broadcasted_iota(I32, (SG_CHUNK, SG_CHUNK), 1)
    for g in range(N_HEADS):
        wg = jnp.where(row >= col, w_ref[g], 0.0).astype(BF16)
        bias = b_ref[g]
        cs = slice(g * SG_CHUNK, (g + 1) * SG_CHUNK)
        for c in range(tm // SG_CHUNK):
            rs = slice(c * SG_CHUNK, (c + 1) * SG_CHUNK)
            mixed = jnp.dot(wg, vn[rs, cs], preferred_element_type=F32) + bias
            o_ref[rs, cs] = (u[rs, cs] * mixed).astype(o_ref.dtype)


def mix_a(z, sg_norm, sg_w, sg_b, tm=512):
    t = z.shape[0]
    bias = jnp.broadcast_to(sg_b[:, :, None], (N_HEADS, SG_CHUNK, SG_CHUNK)).astype(F32)
    return pl.pallas_call(
        _mixa_body,
        grid=(t // tm,),
        in_specs=[pl.BlockSpec((tm, W_A), lambda i: (i, 0)),
                  pl.BlockSpec((1, BRANCH_W), lambda i: (0, 0)),
                  pl.BlockSpec((N_HEADS, SG_CHUNK, SG_CHUNK), lambda i: (0, 0, 0)),
                  pl.BlockSpec((N_HEADS, SG_CHUNK, SG_CHUNK), lambda i: (0, 0, 0))],
        out_specs=pl.BlockSpec((tm, BRANCH_W), lambda i: (i, 0)),
        out_shape=_sds((t, BRANCH_W), BF16),
        compiler_params=_cparams(("arbitrary",)),
        name="mix_a",
    )(z, sg_norm.reshape(1, BRANCH_W), sg_w, bias)


def _mixb_body(bg_ref, cg_ref, hb_ref, w_ref, o_ref, prev_ref):
    @pl.when(pl.program_id(1) == 0)
    def _():
        prev_ref[...] = jnp.zeros_like(prev_ref)

    zz = cg_ref[...].astype(F32) * hb_ref[...].astype(F32)
    tm = zz.shape[0]
    row = lax.broadcasted_iota(I32, zz.shape, 0)
    p = prev_ref[...]
    z1 = jnp.where(row == 0, p[7:8, :], pltpu.roll(zz, 1, 0))
    z2 = jnp.where(row == 0, p[6:7, :], jnp.where(row == 1, p[7:8, :], pltpu.roll(zz, 2, 0)))
    w = w_ref[...]
    y = w[0:1, :] * z2 + w[1:2, :] * z1 + w[2:3, :] * zz
    o_ref[...] = (bg_ref[...].astype(F32) * y).astype(o_ref.dtype)
    prev_ref[...] = zz[tm - 8:, :]


def mix_b(z, conv_w, batch, tm=512):
    t = z.shape[0]
    nt = t // batch // tm
    c0 = W_A // BRANCH_W

    def zspec(k):
        return pl.BlockSpec((tm, BRANCH_W), lambda b, i: (b * nt + i, c0 + k))

    return pl.pallas_call(
        _mixb_body,
        grid=(batch, nt),
        in_specs=[zspec(0), zspec(1), zspec(2), pl.BlockSpec((3, BRANCH_W), lambda b, i: (0, 0))],
        out_specs=pl.BlockSpec((tm, BRANCH_W), lambda b, i: (b * nt + i, 0)),
        out_shape=_sds((t, BRANCH_W), BF16),
        scratch_shapes=[pltpu.VMEM((8, BRANCH_W), F32)],
        compiler_params=_cparams(("arbitrary", "arbitrary")),
        name="mix_b",
    )(z, z, z, conv_w)


def rope_tables(s, half):
    inv = ROPE_THETA ** (-jnp.arange(half, dtype=F32) / half)
    ang = jnp.arange(s, dtype=F32)[:, None] * inv[None, :]
    cos, sin = jnp.cos(ang), jnp.sin(ang)
    reps = LANES // (2 * half)
    cos_t = jnp.tile(jnp.concatenate([cos, cos], axis=-1), (1, reps))
    sin_t = jnp.tile(jnp.concatenate([-sin, sin], axis=-1), (1, reps))
    return cos_t, sin_t


def _prep_c_body(z_ref, gain_ref, cos_ref, sin_ref, o_ref):
    x = z_ref[...].astype(F32)
    gain = gain_ref[0]
    cos = cos_ref[...]
    sin = sin_ref[...]
    for h in range(N_HEADS):
        sl = slice(h * HEAD_DIM, (h + 1) * HEAD_DIM)
        xh = x[:, sl]
        ms = jnp.mean(xh * xh, axis=-1, keepdims=True)
        xn = xh * lax.rsqrt(ms + EPS) * gain
        rot = pltpu.roll(xn, HEAD_DIM // 2, 1)
        o_ref[:, sl] = (xn * cos + rot * sin).astype(o_ref.dtype)


def prep_c(z, qn, kn, cos_t, sin_t, seq, tm=512):
    t = z.shape[0]
    ns = seq // tm
    c0 = (W_A + W_B) // BRANCH_W
    gains = jnp.stack([qn * (HEAD_DIM ** -0.5)] * 3 + [kn] * 3).reshape(6, 1, HEAD_DIM).astype(F32)
    return pl.pallas_call(
        _prep_c_body,
        grid=(t // tm, 6),
        in_specs=[pl.BlockSpec((tm, BRANCH_W), lambda i, j: (i, c0 + j)),
                  pl.BlockSpec((1, 1, HEAD_DIM), lambda i, j: (j, 0, 0)),
                  pl.BlockSpec((tm, LANES), lambda i, j: (i % ns, 0)),
                  pl.BlockSpec((tm, LANES), lambda i, j: (i % ns, 0))],
        out_specs=pl.BlockSpec((tm, BRANCH_W), lambda i, j: (i, j)),
        out_shape=_sds((t, 6 * BRANCH_W), BF16),
        compiler_params=_cparams(("arbitrary", "arbitrary")),
        name="prep_c",
    )(z, gains, cos_t, sin_t)


def _attc_body(q_ref, kc_ref, kp_ref, vc_ref, vp_ref, o_ref, lse_ref):
    i = pl.program_id(1)
    tq = q_ref.shape[1]
    qi = lax.broadcasted_iota(I32, (tq, tq), 0)
    kj = lax.broadcasted_iota(I32, (tq, tq), 1)
    dc = qi - kj
    ok_c = (dc >= 0) & (dc <= N_BACK)
    qp = lax.broadcasted_iota(I32, (tq, N_BACK), 0)
    kp = lax.broadcasted_iota(I32, (tq, N_BACK), 1)
    dp = qp + N_BACK - kp + jnp.where(i > 0, 0, 2 * N_BACK)
    ok_p = dp <= N_BACK
    dn = (((1,), (1,)), ((), ()))
    for h in range(N_HEADS):
        sl = slice(h * HEAD_DIM, (h + 1) * HEAD_DIM)
        q = q_ref[0, :, sl]
        sc = lax.dot_general(q, kc_ref[0, :, sl], dn, preferred_element_type=F32)
        sp = lax.dot_general(q, kp_ref[0, :, sl], dn, preferred_element_type=F32)
        sc = jnp.where(ok_c, sc, NEG)
        sp = jnp.where(ok_p, sp, NEG)
        m = jnp.maximum(jnp.max(sc, axis=-1, keepdims=True), jnp.max(sp, axis=-1, keepdims=True))
        pc = jnp.exp(sc - m)
        pp = jnp.exp(sp - m)
        l = jnp.sum(pc, axis=-1, keepdims=True) + jnp.sum(pp, axis=-1, keepdims=True)
        o = (jnp.dot(pc.astype(BF16), vc_ref[0, :, sl], preferred_element_type=F32)
             + jnp.dot(pp.astype(BF16), vp_ref[0, :, sl], preferred_element_type=F32))
        o_ref[0, :, sl] = (o * (1.0 / l)).astype(o_ref.dtype)
        lse_ref[0, :, sl] = jnp.broadcast_to(m + jnp.log(l), (tq, HEAD_DIM))


def att_c(q, k, v, qcol, kcol, vcol, tq=256):
    nb, length, _ = q.shape
    r = tq // N_BACK

    def cur(c):
        return pl.BlockSpec((1, tq, BRANCH_W), lambda n, i: (n, i, c))

    def prev(c):
        return pl.BlockSpec((1, N_BACK, BRANCH_W), lambda n, i: (n, jnp.maximum(i * r - 1, 0), c))

    return pl.pallas_call(
        _attc_body,
        grid=(nb, length // tq),
        in_specs=[cur(qcol), cur(kcol), prev(kcol), cur(vcol), prev(vcol)],
        out_specs=[pl.BlockSpec((1, tq, BRANCH_W), lambda n, i: (n, i, 0)),
                   pl.BlockSpec((1, tq, BRANCH_W), lambda n, i: (n, i, 0))],
        out_shape=[_sds((nb, length, BRANCH_W), BF16), _sds((nb, length, BRANCH_W), F32)],
        compiler_params=_cparams(("arbitrary", "arbitrary")),
        name="att_c",
    )(q, k, k, v, v)


def _merge_c_body(o1, o2, o3, l1, l2, l3, y_ref):
    a1, a2, a3 = l1[...], l2[...], l3[...]
    m = jnp.maximum(jnp.maximum(a1, a2), a3)
    e1, e2, e3 = jnp.exp(a1 - m), jnp.exp(a2 - m), jnp.exp(a3 - m)
    inv = 1.0 / (e1 + e2 + e3)
    y = e1 * o1[...].astype(F32) + e2 * o2[...].astype(F32) + e3 * o3[...].astype(F32)
    y_ref[...] = (y * inv).astype(y_ref.dtype)


def merge_c(outs, lses, tm=1024):
    t = outs[0].shape[0]
    spec = pl.BlockSpec((tm, BRANCH_W), lambda i: (i, 0))
    return pl.pallas_call(
        _merge_c_body,
        grid=(t // tm,),
        in_specs=[spec] * 6,
        out_specs=spec,
        out_shape=_sds((t, BRANCH_W), BF16),
        compiler_params=_cparams(("arbitrary",)),
        name="merge_c",
    )(*outs, *lses)


def _norm_rope_half(xh, gain, cos, sin, lane):
    sq = xh * xh
    lo = lane < DIFF_DIM
    s0 = jnp.sum(jnp.where(lo, sq, 0.0), axis=-1, keepdims=True)
    s1 = jnp.sum(jnp.where(lo, 0.0, sq), axis=-1, keepdims=True)
    ms = jnp.where(lo, s0, s1) * (1.0 / DIFF_DIM)
    xn = xh * lax.rsqrt(ms + EPS) * gain
    first = (lane % DIFF_DIM) < (DIFF_DIM // 2)
    rot = jnp.where(first, pltpu.roll(xn, LANES - DIFF_DIM // 2, 1), pltpu.roll(xn, DIFF_DIM // 2, 1))
    return xn * cos + rot * sin


def _prep_d_body(zq_ref, zk_ref, gq_ref, gk_ref, cos_ref, sin_ref, q1_ref, q2_ref, k_ref):
    tm = zq_ref.shape[0]
    lane = lax.broadcasted_iota(I32, (tm, LANES), 1)
    lo = lane < DIFF_DIM
    cos = cos_ref[...]
    sin = sin_ref[...]
    zq = zq_ref[...].astype(F32)
    zk = zk_ref[...].astype(F32)
    for h in range(N_HEADS):
        sl = slice(h * HEAD_DIM, (h + 1) * HEAD_DIM)
        q = _norm_rope_half(zq[:, sl], gq_ref[...], cos, sin, lane)
        q1_ref[:, sl] = jnp.where(lo, q, 0.0).astype(BF16)
        q2_ref[:, sl] = jnp.where(lo, 0.0, q).astype(BF16)
        k_ref[:, sl] = _norm_rope_half(zk[:, sl], gk_ref[...], cos, sin, lane).astype(BF16)


def prep_d(z, qn, kn, cos_t, sin_t, seq, tm=512):
    t = z.shape[0]
    ns = seq // tm
    c0 = (W_A + W_B + W_C) // BRANCH_W
    gq = jnp.tile(qn * (DIFF_DIM ** -0.5), 2).reshape(1, LANES).astype(F32)
    gk = jnp.tile(kn, 2).reshape(1, LANES).astype(F32)
    row = pl.BlockSpec((tm, BRANCH_W), lambda i: (i, 0))
    tab = pl.BlockSpec((tm, LANES), lambda i: (i % ns, 0))
    vec = pl.BlockSpec((1, LANES), lambda i: (0, 0))
    return pl.pallas_call(
        _prep_d_body,
        grid=(t // tm,),
        in_specs=[pl.BlockSpec((tm, BRANCH_W), lambda i: (i, c0)),
                  pl.BlockSpec((tm, BRANCH_W), lambda i: (i, c0 + 1)), vec, vec, tab, tab],
        out_specs=[row, row, row],
        out_shape=[_sds((t, BRANCH_W), BF16)] * 3,
        compiler_params=_cparams(("arbitrary",)),
        name="prep_d",
    )(z, z, gq, gk, cos_t, sin_t)


def _diff_body(lam_init, q1_ref, q2_ref, k_ref, v_ref, lq1, lk1, lq2, lk2, sg_ref, o_ref):
    i = pl.program_id(1)
    tq = q1_ref.shape[1]
    lam = (jnp.exp(jnp.sum(lq1[...] * lk1[...], axis=-1, keepdims=True))
           - jnp.exp(jnp.sum(lq2[...] * lk2[...], axis=-1, keepdims=True)) + lam_init)
    row = lax.broadcasted_iota(I32, (tq, tq), 0)
    col = lax.broadcasted_iota(I32, (tq, tq), 1)
    causal = col <= row
    dn = (((1,), (1,)), ((), ()))
    for h in range(N_HEADS):
        sl = slice(h * HEAD_DIM, (h + 1) * HEAD_DIM)
        outs = []
        for q_ref in (q1_ref, q2_ref):
            q = q_ref[0, :, sl]

            def step(j, carry, diag, q=q, sl=sl):
                m, l, acc = carry
                start = pl.multiple_of(j * tq, tq)
                k = k_ref[0, pl.ds(start, tq), sl]
                v = v_ref[0, pl.ds(start, tq), sl]
                s = lax.dot_general(q, k, dn, preferred_element_type=F32)
                if diag:
                    s = jnp.where(causal, s, NEG)
                m_new = jnp.maximum(m, jnp.max(s, axis=-1, keepdims=True))
                a = jnp.exp(m - m_new)
                p = jnp.exp(s - m_new)
                l = a * l + jnp.sum(p, axis=-1, keepdims=True)
                acc = a * acc + jnp.dot(p.astype(BF16), v, preferred_element_type=F32)
                return m_new, l, acc

            init = (jnp.full((tq, 1), NEG, F32), jnp.zeros((tq, 1), F32), jnp.zeros((tq, HEAD_DIM), F32))
            carry = lax.fori_loop(0, i, functools.partial(step, diag=False), init)
            m, l, acc = step(i, carry, True)
            outs.append(acc * (1.0 / l))
        a = outs[0] - lam * outs[1]
        ms = jnp.mean(a * a, axis=-1, keepdims=True)
        o_ref[0, :, sl] = (a * lax.rsqrt(ms + EPS) * sg_ref[...] * (1.0 - lam_init)).astype(o_ref.dtype)


def diff_att(q1, q2, k, zv, vcol, lq1, lk1, lq2, lk2, sub_g, lam_init, tq=256):
    b, s, _ = q1.shape
    qs = pl.BlockSpec((1, tq, BRANCH_W), lambda n, i: (n, i, 0))
    vec = pl.BlockSpec((1, DIFF_DIM), lambda n, i: (0, 0))
    return pl.pallas_call(
        functools.partial(_diff_body, lam_init),
        grid=(b, s // tq),
        in_specs=[qs, qs,
                  pl.BlockSpec((1, s, BRANCH_W), lambda n, i: (n, 0, 0)),
                  pl.BlockSpec((1, s, BRANCH_W), lambda n, i: (n, 0, vcol)),
                  vec, vec, vec, vec,
                  pl.BlockSpec((1, HEAD_DIM), lambda n, i: (0, 0))],
        out_specs=qs,
        out_shape=_sds((b, s, BRANCH_W), BF16),
        compiler_params=_cparams(("arbitrary", "arbitrary")),
        name="diff_att",
    )(q1, q2, k, zv, lq1.reshape(1, -1), lk1.reshape(1, -1), lq2.reshape(1, -1), lk2.reshape(1, -1),
      sub_g.reshape(1, HEAD_DIM))


def _gate_body(h_ref, ya, yb, yc, yd, wg_ref, bg_ref, wb_ref, o_ref):
    h = h_ref[...]
    acc = None
    for i, y in enumerate((ya, yb, yc, yd)):
        g = jax.nn.sigmoid(jnp.dot(h, wg_ref[i], preferred_element_type=F32) + bg_ref[i:i + 1, :])
        t = jnp.dot(y[...], wb_ref[i], preferred_element_type=F32)
        acc = g * t if acc is None else acc + g * t
    o_ref[...] = acc.astype(o_ref.dtype)


def gate_merge(h, ys, w_gate, b_gate, w_branch, tm=512, tn=512):
    t, d = h.shape
    ysp = pl.BlockSpec((tm, BRANCH_W), lambda j, i: (i, 0))
    return pl.pallas_call(
        _gate_body,
        grid=(d // tn, t // tm),
        in_specs=[pl.BlockSpec((tm, d), lambda j, i: (i, 0)), ysp, ysp, ysp, ysp,
                  pl.BlockSpec((4, d, tn), lambda j, i: (0, 0, j)),
                  pl.BlockSpec((4, tn), lambda j, i: (0, j)),
                  pl.BlockSpec((4, BRANCH_W, tn), lambda j, i: (0, 0, j))],
        out_specs=pl.BlockSpec((tm, tn), lambda j, i: (i, j)),
        out_shape=_sds((t, d), BF16),
        compiler_params=_cparams(("arbitrary", "arbitrary")),
        name="gate_merge",
    )(h, *ys, w_gate, b_gate, w_branch)


def _pack_rows(x):
    half = x.shape[1] // 2
    lo = pltpu.bitcast(x[:, :half].astype(BF16).astype(F32), U32)
    hi = pltpu.bitcast(x[:, half:].astype(BF16).astype(F32), U32)
    return (lo >> 16) | hi


def _store_packed(ref, words):
    m = words.shape[0]
    for c in range(PACK_ROWS):
        ref[pl.ds(c, m, stride=PACK_ROWS), :] = words[:, c * LANES:(c + 1) * LANES]


def _load_packed(ref, m):
    words = jnp.concatenate([ref[pl.ds(c, m, stride=PACK_ROWS), :] for c in range(PACK_ROWS)], axis=-1)
    lo = pltpu.bitcast(words << 16, F32)
    hi = pltpu.bitcast(words & jnp.uint32(0xFFFF0000), F32)
    return lo, hi


def _outproj_body(m_ref, x_ref, wo_ref, g_ref, wr_ref, br_ref, x1_ref, hp_ref, meta_ref, cnt_ref, run_ref):
    @pl.when(pl.program_id(0) == 0)
    def _():
        run_ref[...] = jnp.zeros_like(run_ref)

    tm = x_ref.shape[0]
    x1 = x_ref[...] + jnp.dot(m_ref[...], wo_ref[...], preferred_element_type=F32)
    x1_ref[...] = x1
    ms = jnp.mean(x1 * x1, axis=-1, keepdims=True)
    h2 = x1 * lax.rsqrt(ms + EPS) * g_ref[...]
    _store_packed(hp_ref, _pack_rows(h2))

    lg = jnp.dot(h2.astype(BF16), wr_ref[...], preferred_element_type=F32) + br_ref[...]
    lane = lax.broadcasted_iota(I32, (tm, LANES), 1)
    big = jnp.int32(4 * LANES)
    is_g = (lane >= N_EXPERTS) & (lane < N_EXPERTS + N_GROUPS)
    gl = jnp.where(is_g, lg, NEG)
    gm = jnp.max(gl, axis=-1, keepdims=True)
    g_p = 1.0 / jnp.sum(jnp.exp(gl - gm), axis=-1, keepdims=True)
    gidx = jnp.min(jnp.where(gl == gm, lane, big), axis=-1, keepdims=True) - N_EXPERTS
    lo_e = gidx * EXP_PER_GROUP
    in_g = (lane >= lo_e) & (lane < lo_e + EXP_PER_GROUP)
    el = jnp.where(in_g, lg, NEG)
    v1 = jnp.max(el, axis=-1, keepdims=True)
    i1 = jnp.min(jnp.where(el == v1, lane, big), axis=-1, keepdims=True)
    el2 = jnp.where(lane == i1, NEG, el)
    v2 = jnp.max(el2, axis=-1, keepdims=True)
    i2 = jnp.min(jnp.where(el2 == v2, lane, big), axis=-1, keepdims=True)
    e2 = jnp.exp(v2 - v1)
    w1 = g_p / (1.0 + e2)
    w2 = g_p * e2 / (1.0 + e2)

    oh1 = lane == i1
    oh2 = lane == i2
    a = jnp.where(oh1, 1.0, jnp.where(oh2, 1.0, 0.0))
    r = lax.broadcasted_iota(I32, (tm, tm), 0)
    c = lax.broadcasted_iota(I32, (tm, tm), 1)
    before = jnp.where(c < r, 1.0, 0.0).astype(BF16)
    pref = jnp.dot(before, a.astype(BF16), preferred_element_type=F32) + run_ref[0:1, :]
    rank1 = jnp.sum(jnp.where(oh1, pref, 0.0), axis=-1, keepdims=True)
    rank2 = jnp.sum(jnp.where(oh2, pref, 0.0), axis=-1, keepdims=True)
    new_run = run_ref[0:1, :] + jnp.sum(a, axis=0, keepdims=True)
    run_ref[...] = jnp.broadcast_to(new_run, run_ref.shape)
    cnt_ref[...] = jnp.broadcast_to(new_run, cnt_ref.shape)

    meta = jnp.where(lane == 0, i1.astype(F32), 0.0)
    meta = jnp.where(lane == 1, i2.astype(F32), meta)
    meta = jnp.where(lane == 2, w1, meta)
    meta = jnp.where(lane == 3, w2, meta)
    meta = jnp.where(lane == 4, rank1, meta)
    meta = jnp.where(lane == 5, rank2, meta)
    meta_ref[...] = meta


def outproj_route(merged, x, w_out, g2, w_r, b_r, tm=512):
    t, d = x.shape
    return pl.pallas_call(
        _outproj_body,
        grid=(t // tm,),
        in_specs=[pl.BlockSpec((tm, d), lambda i: (i, 0)),
                  pl.BlockSpec((tm, d), lambda i: (i, 0)),
                  pl.BlockSpec((d, d), lambda i: (0, 0)),
                  pl.BlockSpec((1, d), lambda i: (0, 0)),
                  pl.BlockSpec((d, LANES), lambda i: (0, 0)),
                  pl.BlockSpec((1, LANES), lambda i: (0, 0))],
        out_specs=[pl.BlockSpec((tm, d), lambda i: (i, 0)),
                   pl.BlockSpec((tm * PACK_ROWS, LANES), lambda i: (i, 0)),
                   pl.BlockSpec((tm, LANES), lambda i: (i, 0)),
                   pl.BlockSpec((8, LANES), lambda i: (0, 0))],
        out_shape=[_sds((t, d), F32), _sds((t * PACK_ROWS, LANES), U32), _sds((t, LANES), F32),
                   _sds((8, LANES), F32)],
        scratch_shapes=[pltpu.VMEM((8, LANES), F32)],
        compiler_params=_cparams(("arbitrary",)),
        name="outproj_route",
    )(merged, x, w_out, g2.reshape(1, d), w_r, b_r)


def _dispatch_body(p1_ref, p2_ref, hp_ref, xs_ref, sem):
    tm = hp_ref.shape[0] // PACK_ROWS

    def row_copy(t, pos):
        src = hp_ref.at[pl.ds(pl.multiple_of(t * PACK_ROWS, PACK_ROWS), PACK_ROWS), :]
        dst = xs_ref.at[pl.ds(pl.multiple_of(pos * PACK_ROWS, PACK_ROWS), PACK_ROWS), :]
        return pltpu.make_async_copy(src, dst, sem)

    def start(t, carry):
        row_copy(t, p1_ref[0, 0, t]).start()
        row_copy(t, p2_ref[0, 0, t]).start()
        return carry

    def wait(t, carry):
        row_copy(t, p1_ref[0, 0, t]).wait()
        row_copy(t, p2_ref[0, 0, t]).wait()
        return carry

    lax.fori_loop(0, tm, start, 0)
    lax.fori_loop(0, tm, wait, 0)


def dispatch(hp, pos1, pos2, rows, tm=512):
    t = hp.shape[0] // PACK_ROWS
    nt = t // tm
    pspec = pl.BlockSpec((1, 1, tm), lambda i: (i, 0, 0), memory_space=pltpu.SMEM)
    return pl.pallas_call(
        _dispatch_body,
        grid=(nt,),
        in_specs=[pspec, pspec, pl.BlockSpec((tm * PACK_ROWS, LANES), lambda i: (i, 0))],
        out_specs=pl.BlockSpec(memory_space=pl.ANY),
        out_shape=_sds((rows * PACK_ROWS, LANES), U32),
        scratch_shapes=[pltpu.SemaphoreType.DMA(())],
        compiler_params=_cparams(("arbitrary",), has_side_effects=True),
        name="dispatch",
    )(pos1.reshape(nt, 1, tm), pos2.reshape(nt, 1, tm), hp)


def _expert_body(te_ref, tv_ref, xs_ref, wu_ref, wd_ref, ys_ref):
    tr = xs_ref.shape[0] // PACK_ROWS

    @pl.when(tv_ref[pl.program_id(0)] > 0)
    def _():
        lo, hi = _load_packed(xs_ref, tr)
        x = jnp.concatenate([lo, hi], axis=-1).astype(BF16)
        ac = jnp.dot(x, wu_ref[0], preferred_element_type=F32)
        a = ac[:, :D_EXPERT]
        c = ac[:, D_EXPERT:]
        hmid = (a * jax.nn.sigmoid(a) * c).astype(BF16)
        y = jnp.dot(hmid, wd_ref[0], preferred_element_type=F32)
        _store_packed(ys_ref, _pack_rows(y))


def experts(xs, tile_expert, tile_valid, w_up, w_down, tr=ROW_TILE):
    rows = xs.shape[0] // PACK_ROWS
    nt = rows // tr
    grid_spec = pltpu.PrefetchScalarGridSpec(
        num_scalar_prefetch=2,
        grid=(nt,),
        in_specs=[pl.BlockSpec((tr * PACK_ROWS, LANES), lambda i, te, tv: (i, 0)),
                  pl.BlockSpec((1, D_MODEL, 2 * D_EXPERT), lambda i, te, tv: (te[i], 0, 0)),
                  pl.BlockSpec((1, D_EXPERT, D_MODEL), lambda i, te, tv: (te[i], 0, 0))],
        out_specs=pl.BlockSpec((tr * PACK_ROWS, LANES), lambda i, te, tv: (i, 0)),
    )
    return pl.pallas_call(
        _expert_body,
        grid_spec=grid_spec,
        out_shape=_sds((rows * PACK_ROWS, LANES), U32),
        compiler_params=_cparams(("arbitrary",)),
        name="experts",
    )(tile_expert, tile_valid, xs, w_up, w_down)


def _combine_body(p1_ref, p2_ref, ys_ref, x_ref, meta_ref, o_ref, b1, b2, sem):
    tm = x_ref.shape[0]

    def row_copy(t, pos, buf):
        src = ys_ref.at[pl.ds(pl.multiple_of(pos * PACK_ROWS, PACK_ROWS), PACK_ROWS), :]
        dst = buf.at[pl.ds(pl.multiple_of(t * PACK_ROWS, PACK_ROWS), PACK_ROWS), :]
        return pltpu.make_async_copy(src, dst, sem)

    def start(t, carry):
        row_copy(t, p1_ref[0, 0, t], b1).start()
        row_copy(t, p2_ref[0, 0, t], b2).start()
        return carry

    def wait(t, carry):
        row_copy(t, p1_ref[0, 0, t], b1).wait()
        row_copy(t, p2_ref[0, 0, t], b2).wait()
        return carry

    lax.fori_loop(0, tm, start, 0)
    lax.fori_loop(0, tm, wait, 0)
    meta = meta_ref[...]
    w1 = meta[:, 2:3]
    w2 = meta[:, 3:4]
    half = D_MODEL // 2
    lo1, hi1 = _load_packed(b1, tm)
    lo2, hi2 = _load_packed(b2, tm)
    o_ref[:, :half] = x_ref[:, :half] + w1 * lo1 + w2 * lo2
    o_ref[:, half:] = x_ref[:, half:] + w1 * hi1 + w2 * hi2


def combine(ys, pos1, pos2, x1, meta, tm=256):
    t, d = x1.shape
    nt = t // tm
    pspec = pl.BlockSpec((1, 1, tm), lambda i: (i, 0, 0), memory_space=pltpu.SMEM)
    return pl.pallas_call(
        _combine_body,
        grid=(nt,),
        in_specs=[pspec, pspec, pl.BlockSpec(memory_space=pl.ANY),
                  pl.BlockSpec((tm, d), lambda i: (i, 0)),
                  pl.BlockSpec((tm, LANES), lambda i: (i, 0))],
        out_specs=pl.BlockSpec((tm, d), lambda i: (i, 0)),
        out_shape=_sds((t, d), F32),
        scratch_shapes=[pltpu.VMEM((tm * PACK_ROWS, LANES), U32), pltpu.VMEM((tm * PACK_ROWS, LANES), U32),
                        pltpu.SemaphoreType.DMA(())],
        compiler_params=_cparams(("arbitrary",)),
        name="combine",
    )(pos1.reshape(nt, 1, tm), pos2.reshape(nt, 1, tm), ys, x1, meta)


def moe_plan(meta, counts, tr=ROW_TILE):
    t = meta.shape[0]
    n_tiles = (2 * t) // tr + N_EXPERTS
    e1 = meta[:, 0].astype(I32)
    e2 = meta[:, 1].astype(I32)
    cnt = counts[0, :N_EXPERTS].astype(I32)
    tiles = (cnt + tr - 1) // tr
    tile_end = jnp.cumsum(tiles)
    seg_start = (tile_end - tiles) * tr
    pos1 = seg_start[e1] + meta[:, 4].astype(I32)
    pos2 = seg_start[e2] + meta[:, 5].astype(I32)
    tid = jnp.arange(n_tiles, dtype=I32)
    tile_valid = (tid < tile_end[-1]).astype(I32)
    last = jnp.maximum(tile_end[-1] - 1, 0)
    tile_expert = jnp.searchsorted(tile_end, jnp.minimum(tid, last), side="right").astype(I32)
    tile_expert = jnp.minimum(tile_expert, N_EXPERTS - 1)
    return pos1, pos2, tile_expert, tile_valid, n_tiles * tr


def _deinterleave(a, batch, dil):
    t, c = a.shape
    s = t // batch
    return a.reshape(batch, s // dil, dil, c).transpose(0, 2, 1, 3).reshape(batch * dil, s // dil, c)


def _interleave(a, batch, dil):
    nb, length, c = a.shape
    return a.reshape(batch, dil, length, c).transpose(0, 2, 1, 3).reshape(batch * dil * length, c)


def mixer_c(z, qk, batch):
    t = z.shape[0]
    s = t // batch
    vc0 = (W_A + W_B) // BRANCH_W + 6
    outs, lses = [], []
    for g, (_, dil) in enumerate(DIL_CFG):
        if dil == 1:
            o, l = att_c(qk.reshape(batch, s, -1), qk.reshape(batch, s, -1), z.reshape(batch, s, -1),
                         g, 3 + g, vc0 + g)
            outs.append(o.reshape(t, BRANCH_W))
            lses.append(l.reshape(t, BRANCH_W))
        else:
            q = _deinterleave(qk[:, g * BRANCH_W:(g + 1) * BRANCH_W], batch, dil)
            k = _deinterleave(qk[:, (3 + g) * BRANCH_W:(4 + g) * BRANCH_W], batch, dil)
            v = _deinterleave(z[:, (vc0 + g) * BRANCH_W:(vc0 + g + 1) * BRANCH_W], batch, dil)
            o, l = att_c(q, k, v, 0, 0, 0)
            outs.append(_interleave(o, batch, dil))
            lses.append(_interleave(l, batch, dil))
    return merge_c(outs, lses)


def layer(x, p, lam_init, batch, tabs):
    t, d = x.shape
    s = t // batch
    z, h = proj(x, p["norm_mix"], p["w_in"])
    y_a = mix_a(z, p["sg_norm"], p["sg_w"], p["sg_b"])
    y_b = mix_b(z, p["conv_w"], batch)
    qk = prep_c(z, p["qn_c"], p["kn_c"], tabs[0], tabs[1], s)
    y_c = mixer_c(z, qk, batch)
    q1, q2, kd = prep_d(z, p["qn_d"], p["kn_d"], tabs[2], tabs[3], s)
    vcol = (W_A + W_B + W_C) // BRANCH_W + 2
    y_d = diff_att(q1.reshape(batch, s, -1), q2.reshape(batch, s, -1), kd.reshape(batch, s, -1),
                   z.reshape(batch, s, -1), vcol, p["lam_q1"], p["lam_k1"], p["lam_q2"], p["lam_k2"],
                   p["subln_d"], lam_init).reshape(t, BRANCH_W)
    merged = gate_merge(h, (y_a, y_b, y_c, y_d), p["w_gate"], p["b_gate"], p["w_branch"])
    x1, hp, meta, counts = outproj_route(merged, x, p["w_out"], p["norm_ffn"], p["w_r"], p["b_r"])
    pos1, pos2, tile_expert, tile_valid, rows = moe_plan(meta, counts)
    xs = dispatch(hp, pos1, pos2, rows)
    ys = experts(xs, tile_expert, tile_valid, p["w_up"], p["w_down"])
    return combine(ys, pos1, pos2, x1, meta)


def kernel(x, norm_mix, w_in, sg_norm, sg_w, sg_b, conv_w, qn_c, kn_c, qn_d, kn_d, lam_q1, lam_k1, lam_q2,
           lam_k2, subln_d, w_gate, b_gate, w_branch, w_out, norm_ffn, w_rg, b_rg, w_re, b_re, w_up, w_down):
    batch, seq, d = x.shape
    depth = w_in.shape[0]
    tabs = rope_tables(seq, HEAD_DIM // 2) + rope_tables(seq, DIFF_DIM // 2)
    xf = x.reshape(batch * seq, d)
    for l in range(depth):
        pad = LANES - N_EXPERTS - N_GROUPS
        w_r = jnp.concatenate([w_re[l], w_rg[l], jnp.zeros((d, pad), F32)], axis=1).astype(BF16)
        b_r = jnp.concatenate([b_re[l], b_rg[l], jnp.zeros((pad,), F32)]).reshape(1, LANES)
        p = dict(norm_mix=norm_mix[l], w_in=w_in[l].astype(BF16), sg_norm=sg_norm[l], sg_w=sg_w[l], sg_b=sg_b[l],
                 conv_w=conv_w[l], qn_c=qn_c[l], kn_c=kn_c[l], qn_d=qn_d[l], kn_d=kn_d[l], lam_q1=lam_q1[l],
                 lam_k1=lam_k1[l], lam_q2=lam_q2[l], lam_k2=lam_k2[l], subln_d=subln_d[l],
                 w_gate=w_gate[l].astype(BF16), b_gate=b_gate[l], w_branch=w_branch[l].astype(BF16),
                 w_out=w_out[l].astype(BF16), norm_ffn=norm_ffn[l], w_r=w_r, b_r=b_r,
                 w_up=w_up[l].astype(BF16), w_down=w_down[l].astype(BF16))
        xf = layer(xf, p, 0.8 - 0.6 * math.exp(-0.3 * l), batch, tabs)
    return xf.reshape(batch, seq, d)
```

```python
import functools
import math

import jax
import jax.numpy as jnp
from jax import lax
from jax.experimental import pallas as pl
from jax.experimental.pallas import tpu as pltpu

F32 = jnp.float32
BF16 = jnp.bfloat16
I32 = jnp.int32
U32 = jnp.uint32

D_MODEL = 2048
BRANCH_W = 512
HEAD_DIM = 128
N_HEADS = 4
SG_CHUNK = 128
DIL_CFG = ((128, 1), (512, 4), (2048, 16))
N_BACK = 128
DIFF_DIM = 64
ROPE_THETA = 10000.0
W_A = 2 * BRANCH_W
W_B = 3 * BRANCH_W
W_C = 9 * BRANCH_W
N_GROUPS = 4
EXP_PER_GROUP = 8
N_EXPERTS = 32
D_EXPERT = 512
EPS = 1e-6
NEG = -1e30

LANES = 128
PACK_ROWS = D_MODEL // 2 // LANES
VMEM_LIMIT = 56 * 1024 * 1024
ROW_TILE = 256
DIFF_TILE = 256
ROUTE_TILE = 512


def _cparams(sem, **kw):
    return pltpu.CompilerParams(dimension_semantics=sem, vmem_limit_bytes=VMEM_LIMIT, **kw)


def _sds(shape, dtype):
    return jax.ShapeDtypeStruct(shape, dtype)


def _proj_body(x_ref, g_ref, w_ref, z_ref, h_ref):
    @pl.when(pl.program_id(1) == 0)
    def _():
        x = x_ref[...]
        ms = jnp.mean(x * x, axis=-1, keepdims=True)
        h_ref[...] = (x * lax.rsqrt(ms + EPS) * g_ref[...]).astype(BF16)

    z_ref[...] = jnp.dot(h_ref[...], w_ref[...], preferred_element_type=F32).astype(z_ref.dtype)


def proj(x, g, w, tm=1024, tn=512):
    t, d = x.shape
    n = w.shape[1]
    return pl.pallas_call(
        _proj_body,
        grid=(t // tm, n // tn),
        in_specs=[pl.BlockSpec((tm, d), lambda i, j: (i, 0)),
                  pl.BlockSpec((1, d), lambda i, j: (0, 0)),
                  pl.BlockSpec((d, tn), lambda i, j: (0, j))],
        out_specs=[pl.BlockSpec((tm, tn), lambda i, j: (i, j)),
                   pl.BlockSpec((tm, d), lambda i, j: (i, 0))],
        out_shape=[_sds((t, n), BF16), _sds((t, d), BF16)],
        compiler_params=_cparams(("arbitrary", "arbitrary")),
        name="proj",
    )(x, g.reshape(1, d), w)


def _gelu_tanh(x):
    c = math.sqrt(2.0 / math.pi)
    return 0.5 * x * (1.0 + jnp.tanh(c * (x + 0.044715 * (x * x * x))))


def _mixa_body(z_ref, g_ref, w_ref, b_ref, o_ref):
    tm = z_ref.shape[0]
    ga = _gelu_tanh(z_ref[...].astype(F32))
    u = ga[:, :BRANCH_W]
    v = ga[:, BRANCH_W:]
    ms = jnp.mean(v * v, axis=-1, keepdims=True)
    vn = (v * lax.rsqrt(ms + EPS) * g_ref[...]).astype(BF16)
    row = lax.broadcasted_iota(I32, (SG_CHUNK, SG_CHUNK), 0)
    col = lax.broadcasted_iota(I32, (SG_CHUNK, SG_CHUNK), 1)
    for g in range(N_HEADS):
        wg = jnp.where(row >= col, w_ref[g], 0.0).astype(BF16)
        bias = b_ref[g]
        cs = slice(g * SG_CHUNK, (g + 1) * SG_CHUNK)
        for c in range(tm // SG_CHUNK):
            rs = slice(c * SG_CHUNK, (c + 1) * SG_CHUNK)
            mixed = jnp.dot(wg, vn[rs, cs], preferred_element_type=F32) + bias
            o_ref[rs, cs] = (u[rs, cs] * mixed).astype(o_ref.dtype)


def mix_a(z, sg_norm, sg_w, sg_b, tm=512):
    t = z.shape[0]
    bias = jnp.broadcast_to(sg_b[:, :, None], (N_HEADS, SG_CHUNK, SG_CHUNK)).astype(F32)
    return pl.pallas_call(
        _mixa_body,
        grid=(t // tm,),
        in_specs=[pl.BlockSpec((tm, W_A), lambda i: (i, 0)),
                  pl.BlockSpec((1, BRANCH_W), lambda i: (0, 0)),
                  pl.BlockSpec((N_HEADS, SG_CHUNK, SG_CHUNK), lambda i: (0, 0, 0)),
                  pl.BlockSpec((N_HEADS, SG_CHUNK, SG_CHUNK), lambda i: (0, 0, 0))],
        out_specs=pl.BlockSpec((tm, BRANCH_W), lambda i: (i, 0)),
        out_shape=_sds((t, BRANCH_W), BF16),
        compiler_params=_cparams(("arbitrary",)),
        name="mix_a",
    )(z, sg_norm.reshape(1, BRANCH_W), sg_w, bias)


def _mixb_body(bg_ref, cg_ref, hb_ref, w_ref, o_ref, prev_ref):
    @pl.when(pl.program_id(1) == 0)
    def _():
        prev_ref[...] = jnp.zeros_like(prev_ref)

    zz = cg_ref[...].astype(F32) * hb_ref[...].astype(F32)
    tm = zz.shape[0]
    row = lax.broadcasted_iota(I32, zz.shape, 0)
    p = prev_ref[...]
    z1 = jnp.where(row == 0, p[7:8, :], pltpu.roll(zz, 1, 0))
    z2 = jnp.where(row == 0, p[6:7, :], jnp.where(row == 1, p[7:8, :], pltpu.roll(zz, 2, 0)))
    w = w_ref[...]
    y = w[0:1, :] * z2 + w[1:2, :] * z1 + w[2:3, :] * zz
    o_ref[...] = (bg_ref[...].astype(F32) * y).astype(o_ref.dtype)
    prev_ref[...] = zz[tm - 8:, :]


def mix_b(z, conv_w, batch, tm=512):
    t = z.shape[0]
    nt = t // batch // tm
    c0 = W_A // BRANCH_W

    def zspec(k):
        return pl.BlockSpec((tm, BRANCH_W), lambda b, i: (b * nt + i, c0 + k))

    return pl.pallas_call(
        _mixb_body,
        grid=(batch, nt),
        in_specs=[zspec(0), zspec(1), zspec(2), pl.BlockSpec((3, BRANCH_W), lambda b, i: (0, 0))],
        out_specs=pl.BlockSpec((tm, BRANCH_W), lambda b, i: (b * nt + i, 0)),
        out_shape=_sds((t, BRANCH_W), BF16),
        scratch_shapes=[pltpu.VMEM((8, BRANCH_W), F32)],
        compiler_params=_cparams(("arbitrary", "arbitrary")),
        name="mix_b",
    )(z, z, z, conv_w)


def rope_tables(s, half):
    inv = ROPE_THETA ** (-jnp.arange(half, dtype=F32) / half)
    ang = jnp.arange(s, dtype=F32)[:, None] * inv[None, :]
    cos, sin = jnp.cos(ang), jnp.sin(ang)
    reps = LANES // (2 * half)
    cos_t = jnp.tile(jnp.concatenate([cos, cos], axis=-1), (1, reps))
    sin_t = jnp.tile(jnp.concatenate([-sin, sin], axis=-1), (1, reps))
    return cos_t, sin_t


def _prep_c_body(dil, zq_ref, zk_ref, zv_ref, gq_ref, gk_ref, cos_ref, sin_ref, o_ref, scr_ref):
    tm = zq_ref.shape[0]
    cos = cos_ref[...]
    sin = sin_ref[...]

    def norm_rope(xh, gain):
        ms = jnp.mean(xh * xh, axis=-1, keepdims=True)
        xn = xh * lax.rsqrt(ms + EPS) * gain
        return xn * cos + pltpu.roll(xn, HEAD_DIM // 2, 1) * sin

    width = 3 * BRANCH_W
    for w, ref in enumerate((zq_ref, zk_ref, zv_ref)):
        for h in range(N_HEADS):
            xh = ref[:, h * HEAD_DIM:(h + 1) * HEAD_DIM].astype(F32)
            val = xh if w == 2 else norm_rope(xh, (gq_ref, gk_ref)[w][...])
            c = w * BRANCH_W + h * HEAD_DIM
            if dil == 1:
                o_ref[0, :, c:c + HEAD_DIM] = val.astype(o_ref.dtype)
            else:
                buf = scr_ref.at[w * N_HEADS + h]
                buf[...] = val
                for r in range(dil):
                    o_ref[0, :, r * width + c:r * width + c + HEAD_DIM] = (
                        buf[pl.ds(r, tm // dil, stride=dil), :].astype(o_ref.dtype))


def prep_c(z, g, dil, qn, kn, cos_t, sin_t, batch, tm=512):
    t = z.shape[0]
    seq = t // batch
    ns = seq // tm
    c0 = (W_A + W_B) // BRANCH_W + g
    gq = (qn * (HEAD_DIM ** -0.5 * math.log2(math.e))).reshape(1, HEAD_DIM).astype(F32)
    gk = kn.reshape(1, HEAD_DIM).astype(F32)
    vec = pl.BlockSpec((1, HEAD_DIM), lambda i: (0, 0))
    tab = pl.BlockSpec((tm, LANES), lambda i: (i % ns, 0))
    return pl.pallas_call(
        functools.partial(_prep_c_body, dil),
        grid=(t // tm,),
        in_specs=[pl.BlockSpec((tm, BRANCH_W), lambda i: (i, c0)),
                  pl.BlockSpec((tm, BRANCH_W), lambda i: (i, c0 + 3)),
                  pl.BlockSpec((tm, BRANCH_W), lambda i: (i, c0 + 6)), vec, vec, tab, tab],
        out_specs=pl.BlockSpec((1, tm // dil, dil * 3 * BRANCH_W), lambda i: (i // ns, i % ns, 0)),
        out_shape=_sds((batch, seq // dil, dil * 3 * BRANCH_W), BF16),
        scratch_shapes=[pltpu.VMEM((3 * N_HEADS, tm, HEAD_DIM), F32)],
        compiler_params=_cparams(("arbitrary",)),
        name="prep_c",
    )(z, z, z, gq, gk, cos_t, sin_t)


def _attc_body(q_ref, kc_ref, kp_ref, vc_ref, vp_ref, o_ref, lse_ref):
    i = pl.program_id(1)
    nsub = q_ref.shape[1] // N_BACK
    qi = lax.broadcasted_iota(I32, (N_BACK, N_BACK), 0)
    kj = lax.broadcasted_iota(I32, (N_BACK, N_BACK), 1)
    ok_own = kj <= qi
    ok_before = kj >= qi
    ok_first = (kj + jnp.where(i > 0, 0, -2 * N_BACK)) >= qi
    dn = (((1,), (1,)), ((), ()))
    heads = [slice(h * HEAD_DIM, (h + 1) * HEAD_DIM) for h in range(N_HEADS)]
    subs = [slice(u * N_BACK, (u + 1) * N_BACK) for u in range(nsub)]
    scores = []
    for h in heads:
        for u in range(nsub):
            q = q_ref[0, subs[u], h]
            k_before = kp_ref[0, :, h] if u == 0 else kc_ref[0, subs[u - 1], h]
            s_before = lax.dot_general(q, k_before, dn, preferred_element_type=F32)
            s_own = lax.dot_general(q, kc_ref[0, subs[u], h], dn, preferred_element_type=F32)
            scores.append((jnp.where(ok_first if u == 0 else ok_before, s_before, NEG),
                           jnp.where(ok_own, s_own, NEG)))
    probs = []
    for s_before, s_own in scores:
        m = jnp.maximum(jnp.max(s_before, axis=-1, keepdims=True), jnp.max(s_own, axis=-1, keepdims=True))
        p_before = jnp.exp2(s_before - m)
        p_own = jnp.exp2(s_own - m)
        l = jnp.sum(p_before, axis=-1, keepdims=True) + jnp.sum(p_own, axis=-1, keepdims=True)
        probs.append((p_before.astype(BF16), p_own.astype(BF16), m, l))
    n = 0
    for h in heads:
        for u in range(nsub):
            p_before, p_own, m, l = probs[n]
            n += 1
            v_before = vp_ref[0, :, h] if u == 0 else vc_ref[0, subs[u - 1], h]
            o = (jnp.dot(p_before, v_before, preferred_element_type=F32)
                 + jnp.dot(p_own, vc_ref[0, subs[u], h], preferred_element_type=F32))
            o_ref[0, subs[u], h] = (o * (1.0 / l)).astype(o_ref.dtype)
            lse_ref[0, subs[u], h] = jnp.broadcast_to(m + jnp.log2(l), (N_BACK, HEAD_DIM))


def att_c(qkv, dil, tq=256):
    b, length, _ = qkv.shape
    r = tq // N_BACK

    def cur(w):
        return pl.BlockSpec((1, tq, BRANCH_W), lambda n, i: (n // dil, i, (n % dil) * 3 + w))

    def prev(w):
        return pl.BlockSpec((1, N_BACK, BRANCH_W),
                            lambda n, i: (n // dil, jnp.maximum(i * r - 1, 0), (n % dil) * 3 + w))

    out = pl.BlockSpec((1, tq, BRANCH_W), lambda n, i: (n // dil, i, n % dil))
    return pl.pallas_call(
        _attc_body,
        grid=(b * dil, length // tq),
        in_specs=[cur(0), cur(1), prev(1), cur(2), prev(2)],
        out_specs=[out, out],
        out_shape=[_sds((b, length, dil * BRANCH_W), BF16), _sds((b, length, dil * BRANCH_W), F32)],
        compiler_params=_cparams(("arbitrary", "arbitrary")),
        name="att_c",
    )(qkv, qkv, qkv, qkv, qkv)


def _merge_c_body(dils, *refs):
    n = len(dils)
    o_refs, l_refs, y_ref, scr = refs[:n], refs[n:2 * n], refs[2 * n], refs[2 * n + 1:]
    tm = y_ref.shape[0]

    def natural(ref, dil, buf):
        if dil == 1:
            return ref[0].astype(F32)
        for h in range(N_HEADS):
            for r in range(dil):
                c = r * BRANCH_W + h * HEAD_DIM
                buf[h, pl.ds(r, tm // dil, stride=dil), :] = ref[0, :, c:c + HEAD_DIM].astype(F32)
        return jnp.concatenate([buf[h] for h in range(N_HEADS)], axis=-1)

    outs = [natural(o_refs[g], dils[g], scr[2 * g]) for g in range(n)]
    lses = [natural(l_refs[g], dils[g], scr[2 * g + 1]) for g in range(n)]
    m = functools.reduce(jnp.maximum, lses)
    es = [jnp.exp2(l - m) for l in lses]
    y = functools.reduce(lambda a, b: a + b, [e * o for e, o in zip(es, outs)])
    y_ref[...] = (y / functools.reduce(lambda a, b: a + b, es)).astype(y_ref.dtype)


def merge_c(outs, lses, dils, tm=512):
    batch, seq = outs[0].shape[0], outs[0].shape[1] * dils[0]
    t = batch * seq
    ns = seq // tm
    specs = [pl.BlockSpec((1, tm // d, d * BRANCH_W), lambda i: (i // ns, i % ns, 0)) for d in dils]
    return pl.pallas_call(
        functools.partial(_merge_c_body, tuple(dils)),
        grid=(t // tm,),
        in_specs=specs + specs,
        out_specs=pl.BlockSpec((tm, BRANCH_W), lambda i: (i, 0)),
        out_shape=_sds((t, BRANCH_W), BF16),
        scratch_shapes=[pltpu.VMEM((N_HEADS, tm, HEAD_DIM), F32) for _ in range(2 * len(dils))],
        compiler_params=_cparams(("arbitrary",)),
        name="merge_c",
    )(*outs, *lses)


def _norm_rope_half(xh, gain, cos, sin, lane):
    sq = xh * xh
    lo = lane < DIFF_DIM
    s0 = jnp.sum(jnp.where(lo, sq, 0.0), axis=-1, keepdims=True)
    s1 = jnp.sum(jnp.where(lo, 0.0, sq), axis=-1, keepdims=True)
    ms = jnp.where(lo, s0, s1) * (1.0 / DIFF_DIM)
    xn = xh * lax.rsqrt(ms + EPS) * gain
    first = (lane % DIFF_DIM) < (DIFF_DIM // 2)
    rot = jnp.where(first, pltpu.roll(xn, LANES - DIFF_DIM // 2, 1), pltpu.roll(xn, DIFF_DIM // 2, 1))
    return xn * cos + rot * sin


def _prep_d_body(zq_ref, zk_ref, zv_ref, gq_ref, gk_ref, cos_ref, sin_ref, q1_ref, q2_ref, k_ref, vt_ref):
    tm = zq_ref.shape[0]
    lane = lax.broadcasted_iota(I32, (tm, LANES), 1)
    lo = lane < DIFF_DIM
    cos = cos_ref[...]
    sin = sin_ref[...]
    zq = zq_ref[...].astype(F32)
    zk = zk_ref[...].astype(F32)
    for h in range(N_HEADS):
        sl = slice(h * HEAD_DIM, (h + 1) * HEAD_DIM)
        q = _norm_rope_half(zq[:, sl], gq_ref[...], cos, sin, lane)
        q1_ref[:, sl] = jnp.where(lo, q, 0.0).astype(BF16)
        q2_ref[:, sl] = jnp.where(lo, 0.0, q).astype(BF16)
        k_ref[:, sl] = _norm_rope_half(zk[:, sl], gk_ref[...], cos, sin, lane).astype(BF16)
    vt_ref[0] = zv_ref[...].astype(F32).T.astype(BF16)


def prep_d(z, qn, kn, cos_t, sin_t, seq, tm=DIFF_TILE):
    t = z.shape[0]
    ns = seq // tm
    c0 = (W_A + W_B + W_C) // BRANCH_W
    gq = jnp.tile(qn * (DIFF_DIM ** -0.5 * math.log2(math.e)), 2).reshape(1, LANES).astype(F32)
    gk = jnp.tile(kn, 2).reshape(1, LANES).astype(F32)
    row = pl.BlockSpec((tm, BRANCH_W), lambda i: (i, 0))
    tab = pl.BlockSpec((tm, LANES), lambda i: (i % ns, 0))
    vec = pl.BlockSpec((1, LANES), lambda i: (0, 0))
    return pl.pallas_call(
        _prep_d_body,
        grid=(t // tm,),
        in_specs=[pl.BlockSpec((tm, BRANCH_W), lambda i: (i, c0)),
                  pl.BlockSpec((tm, BRANCH_W), lambda i: (i, c0 + 1)),
                  pl.BlockSpec((tm, BRANCH_W), lambda i: (i, c0 + 2)), vec, vec, tab, tab],
        out_specs=[row, row, row, pl.BlockSpec((1, BRANCH_W, tm), lambda i: (i, 0, 0))],
        out_shape=[_sds((t, BRANCH_W), BF16)] * 3 + [_sds((t // tm, BRANCH_W, tm), BF16)],
        compiler_params=_cparams(("arbitrary",)),
        name="prep_d",
    )(z, z, z, gq, gk, cos_t, sin_t)


def _diff_body(lam_init, q1_ref, q2_ref, k_ref, vt_ref, lq1, lk1, lq2, lk2, sg_ref, o_ref, acc_ref):
    i = pl.program_id(1)
    tq = q1_ref.shape[1]
    n_chain = 2 * N_HEADS
    lam = (jnp.exp(jnp.sum(lq1[...] * lk1[...], axis=-1, keepdims=True))
           - jnp.exp(jnp.sum(lq2[...] * lk2[...], axis=-1, keepdims=True)) + lam_init)
    krow = lax.broadcasted_iota(I32, (tq, tq), 0)
    qcol = lax.broadcasted_iota(I32, (tq, tq), 1)
    causal = krow <= qcol
    dn = (((1,), (1,)), ((), ()))
    acc_ref[...] = jnp.zeros_like(acc_ref)

    def step(j, carry, diag):
        ms, ls = carry
        start = pl.multiple_of(j * tq, tq)
        heads = [slice(h * HEAD_DIM, (h + 1) * HEAD_DIM) for h in range(N_HEADS)]
        scores = []
        for c in range(n_chain):
            q = (q1_ref, q2_ref)[c % 2][0, :, heads[c // 2]]
            scores.append(lax.dot_general(k_ref[0, pl.ds(start, tq), heads[c // 2]], q, dn,
                                          preferred_element_type=F32))
        new_m, new_l, alphas, probs = [], [], [], []
        for c in range(n_chain):
            s = jnp.where(causal, scores[c], NEG) if diag else scores[c]
            m_new = jnp.maximum(ms[c], jnp.max(s, axis=0, keepdims=True))
            a = jnp.exp2(ms[c] - m_new)
            p = jnp.exp2(s - m_new)
            new_l.append(a * ls[c] + jnp.sum(p, axis=0, keepdims=True))
            new_m.append(m_new)
            alphas.append(a)
            probs.append(p.astype(BF16))
        for c in range(n_chain):
            pv = jnp.dot(vt_ref[j, heads[c // 2], :], probs[c], preferred_element_type=F32)
            acc_ref[c] = alphas[c] * acc_ref[c] + pv
        return tuple(new_m), tuple(new_l)

    init = (tuple(jnp.full((1, tq), NEG, F32) for _ in range(n_chain)),
            tuple(jnp.zeros((1, tq), F32) for _ in range(n_chain)))
    carry = lax.fori_loop(0, i, functools.partial(step, diag=False), init)
    _, ls = step(i, carry, True)
    for h in range(N_HEADS):
        o1 = acc_ref[2 * h] * (1.0 / ls[2 * h])
        o2 = acc_ref[2 * h + 1] * (1.0 / ls[2 * h + 1])
        a = o1 - lam * o2
        ms = jnp.mean(a * a, axis=0, keepdims=True)
        y = a * lax.rsqrt(ms + EPS) * sg_ref[...]
        o_ref[0, :, h * HEAD_DIM:(h + 1) * HEAD_DIM] = y.T.astype(o_ref.dtype)


def diff_att(q1, q2, k, vt, lq1, lk1, lq2, lk2, sub_g, lam_init, tq=DIFF_TILE):
    b, s, _ = q1.shape
    nk = s // tq
    qs = pl.BlockSpec((1, tq, BRANCH_W), lambda n, i: (n, i, 0))
    vec = pl.BlockSpec((1, DIFF_DIM), lambda n, i: (0, 0))
    sg = jnp.broadcast_to((sub_g * (1.0 - lam_init))[:, None], (HEAD_DIM, tq)).astype(F32)
    return pl.pallas_call(
        functools.partial(_diff_body, lam_init),
        grid=(b, nk),
        in_specs=[qs, qs,
                  pl.BlockSpec((1, s, BRANCH_W), lambda n, i: (n, 0, 0)),
                  pl.BlockSpec((nk, BRANCH_W, tq), lambda n, i: (n, 0, 0)),
                  vec, vec, vec, vec,
                  pl.BlockSpec((HEAD_DIM, tq), lambda n, i: (0, 0))],
        out_specs=qs,
        out_shape=_sds((b, s, BRANCH_W), BF16),
        scratch_shapes=[pltpu.VMEM((2 * N_HEADS, HEAD_DIM, tq), F32)],
        compiler_params=_cparams(("arbitrary", "arbitrary")),
        name="diff_att",
    )(q1, q2, k, vt, lq1.reshape(1, -1), lk1.reshape(1, -1), lq2.reshape(1, -1), lk2.reshape(1, -1), sg)


def _gate_body(h_ref, ya, yb, yc, yd, wg_ref, bg_ref, wb_ref, o_ref):
    h = h_ref[...]
    acc = None
    for i, y in enumerate((ya, yb, yc, yd)):
        g = jax.nn.sigmoid(jnp.dot(h, wg_ref[i], preferred_element_type=F32) + bg_ref[i:i + 1, :])
        t = jnp.dot(y[...], wb_ref[i], preferred_element_type=F32)
        acc = g * t if acc is None else acc + g * t
    o_ref[...] = acc.astype(o_ref.dtype)


def gate_merge(h, ys, w_gate, b_gate, w_branch, tm=512, tn=512):
    t, d = h.shape
    ysp = pl.BlockSpec((tm, BRANCH_W), lambda j, i: (i, 0))
    return pl.pallas_call(
        _gate_body,
        grid=(d // tn, t // tm),
        in_specs=[pl.BlockSpec((tm, d), lambda j, i: (i, 0)), ysp, ysp, ysp, ysp,
                  pl.BlockSpec((4, d, tn), lambda j, i: (0, 0, j)),
                  pl.BlockSpec((4, tn), lambda j, i: (0, j)),
                  pl.BlockSpec((4, BRANCH_W, tn), lambda j, i: (0, 0, j))],
        out_specs=pl.BlockSpec((tm, tn), lambda j, i: (i, j)),
        out_shape=_sds((t, d), BF16),
        compiler_params=_cparams(("arbitrary", "arbitrary")),
        name="gate_merge",
    )(h, *ys, w_gate, b_gate, w_branch)


def _pack_rows(x):
    half = x.shape[1] // 2
    lo = pltpu.bitcast(x[:, :half].astype(BF16).astype(F32), U32)
    hi = pltpu.bitcast(x[:, half:].astype(BF16).astype(F32), U32)
    return (lo >> 16) | hi


def _store_packed(ref, words):
    m = words.shape[0]
    for c in range(PACK_ROWS):
        ref[pl.ds(c, m, stride=PACK_ROWS), :] = words[:, c * LANES:(c + 1) * LANES]


def _load_packed(ref, m):
    words = jnp.concatenate([ref[pl.ds(c, m, stride=PACK_ROWS), :] for c in range(PACK_ROWS)], axis=-1)
    lo = pltpu.bitcast(words << 16, F32)
    hi = pltpu.bitcast(words & jnp.uint32(0xFFFF0000), F32)
    return lo, hi


def _outproj_body(m_ref, x_ref, wo_ref, g_ref, wr_ref, br_ref, x1_ref, hp_ref, meta_ref, cnt_ref, imeta_ref,
                  run_ref):
    @pl.when(pl.program_id(0) == 0)
    def _():
        run_ref[...] = jnp.zeros_like(run_ref)

    tm = x_ref.shape[0]
    x1 = x_ref[...] + jnp.dot(m_ref[...], wo_ref[...], preferred_element_type=F32)
    x1_ref[...] = x1
    ms = jnp.mean(x1 * x1, axis=-1, keepdims=True)
    h2 = x1 * lax.rsqrt(ms + EPS) * g_ref[...]
    _store_packed(hp_ref, _pack_rows(h2))

    lg = jnp.dot(h2.astype(BF16), wr_ref[...], preferred_element_type=F32) + br_ref[...]
    lane = lax.broadcasted_iota(I32, (tm, LANES), 1)
    big = jnp.int32(4 * LANES)
    is_g = (lane >= N_EXPERTS) & (lane < N_EXPERTS + N_GROUPS)
    gl = jnp.where(is_g, lg, NEG)
    gm = jnp.max(gl, axis=-1, keepdims=True)
    g_p = 1.0 / jnp.sum(jnp.exp(gl - gm), axis=-1, keepdims=True)
    gidx = jnp.min(jnp.where(gl == gm, lane, big), axis=-1, keepdims=True) - N_EXPERTS
    lo_e = gidx * EXP_PER_GROUP
    in_g = (lane >= lo_e) & (lane < lo_e + EXP_PER_GROUP)
    el = jnp.where(in_g, lg, NEG)
    v1 = jnp.max(el, axis=-1, keepdims=True)
    i1 = jnp.min(jnp.where(el == v1, lane, big), axis=-1, keepdims=True)
    el2 = jnp.where(lane == i1, NEG, el)
    v2 = jnp.max(el2, axis=-1, keepdims=True)
    i2 = jnp.min(jnp.where(el2 == v2, lane, big), axis=-1, keepdims=True)
    e2 = jnp.exp(v2 - v1)
    w1 = g_p / (1.0 + e2)
    w2 = g_p * e2 / (1.0 + e2)

    oh1 = lane == i1
    oh2 = lane == i2
    a = jnp.where(oh1, 1.0, jnp.where(oh2, 1.0, 0.0))
    r = lax.broadcasted_iota(I32, (tm, tm), 0)
    c = lax.broadcasted_iota(I32, (tm, tm), 1)
    before = jnp.where(c < r, 1.0, 0.0).astype(BF16)
    pref = jnp.dot(before, a.astype(BF16), preferred_element_type=F32) + run_ref[0:1, :]
    rank1 = jnp.sum(jnp.where(oh1, pref, 0.0), axis=-1, keepdims=True)
    rank2 = jnp.sum(jnp.where(oh2, pref, 0.0), axis=-1, keepdims=True)
    new_run = run_ref[0:1, :] + jnp.sum(a, axis=0, keepdims=True)
    run_ref[...] = jnp.broadcast_to(new_run, run_ref.shape)
    cnt_ref[...] = jnp.broadcast_to(new_run, cnt_ref.shape)

    meta = jnp.where(lane == 0, i1.astype(F32), 0.0)
    meta = jnp.where(lane == 1, i2.astype(F32), meta)
    meta = jnp.where(lane == 2, w1, meta)
    meta = jnp.where(lane == 3, w2, meta)
    meta = jnp.where(lane == 4, rank1, meta)
    meta = jnp.where(lane == 5, rank2, meta)
    meta_ref[...] = meta
    imeta_ref[0] = meta.T[0:8, :].astype(I32)


def outproj_route(merged, x, w_out, g2, w_r, b_r, tm=ROUTE_TILE):
    t, d = x.shape
    return pl.pallas_call(
        _outproj_body,
        grid=(t // tm,),
        in_specs=[pl.BlockSpec((tm, d), lambda i: (i, 0)),
                  pl.BlockSpec((tm, d), lambda i: (i, 0)),
                  pl.BlockSpec((d, d), lambda i: (0, 0)),
                  pl.BlockSpec((1, d), lambda i: (0, 0)),
                  pl.BlockSpec((d, LANES), lambda i: (0, 0)),
                  pl.BlockSpec((1, LANES), lambda i: (0, 0))],
        out_specs=[pl.BlockSpec((tm, d), lambda i: (i, 0)),
                   pl.BlockSpec((tm * PACK_ROWS, LANES), lambda i: (i, 0)),
                   pl.BlockSpec((tm, LANES), lambda i: (i, 0)),
                   pl.BlockSpec((8, LANES), lambda i: (0, 0)),
                   pl.BlockSpec((1, 8, tm), lambda i: (i, 0, 0))],
        out_shape=[_sds((t, d), F32), _sds((t * PACK_ROWS, LANES), U32), _sds((t, LANES), F32),
                   _sds((8, LANES), F32), _sds((t // tm, 8, tm), I32)],
        scratch_shapes=[pltpu.VMEM((8, LANES), F32)],
        compiler_params=_cparams(("arbitrary",)),
        name="outproj_route",
    )(merged, x, w_out, g2.reshape(1, d), w_r, b_r)


def _slot(seg_ref, imeta_ref, k, t):
    return seg_ref[imeta_ref[0, k, t]] + imeta_ref[0, 4 + k, t]


def _dispatch_body(seg_ref, imeta_ref, hp_ref, xs_ref, sem):
    tm = hp_ref.shape[0] // PACK_ROWS

    def row_copy(t, k):
        pos = _slot(seg_ref, imeta_ref, k, t)
        src = hp_ref.at[pl.ds(pl.multiple_of(t * PACK_ROWS, PACK_ROWS), PACK_ROWS), :]
        dst = xs_ref.at[pl.ds(pl.multiple_of(pos * PACK_ROWS, PACK_ROWS), PACK_ROWS), :]
        return pltpu.make_async_copy(src, dst, sem)

    def start(t, carry):
        row_copy(t, 0).start()
        row_copy(t, 1).start()
        return carry

    lax.fori_loop(0, tm, start, 0)
    whole = pltpu.make_async_copy(hp_ref, xs_ref.at[pl.ds(0, tm * PACK_ROWS), :], sem)
    whole.wait()
    whole.wait()


def dispatch(hp, seg_start, imeta, rows):
    nt, _, tm = imeta.shape
    grid_spec = pltpu.PrefetchScalarGridSpec(
        num_scalar_prefetch=1,
        grid=(nt,),
        in_specs=[pl.BlockSpec((1, 8, tm), lambda i, seg: (i, 0, 0), memory_space=pltpu.SMEM),
                  pl.BlockSpec((tm * PACK_ROWS, LANES), lambda i, seg: (i, 0))],
        out_specs=pl.BlockSpec(memory_space=pl.ANY),
        scratch_shapes=[pltpu.SemaphoreType.DMA(())],
    )
    return pl.pallas_call(
        _dispatch_body,
        grid_spec=grid_spec,
        out_shape=_sds((rows * PACK_ROWS, LANES), U32),
        compiler_params=_cparams(("arbitrary",), has_side_effects=True),
        name="dispatch",
    )(seg_start, imeta, hp)


def _expert_body(te_ref, tv_ref, xs_ref, wu_ref, wd_ref, ys_ref, wu_bf, wd_bf):
    i = pl.program_id(0)
    tr = xs_ref.shape[0] // PACK_ROWS

    @pl.when(jnp.logical_or(i == 0, te_ref[i] != te_ref[jnp.maximum(i - 1, 0)]))
    def _():
        wu_bf[...] = wu_ref[0, 0].astype(BF16)
        wd_bf[...] = wd_ref[0, 0].astype(BF16)

    @pl.when(tv_ref[i] > 0)
    def _():
        lo, hi = _load_packed(xs_ref, tr)
        x = jnp.concatenate([lo, hi], axis=-1).astype(BF16)
        ac = jnp.dot(x, wu_bf[...], preferred_element_type=F32)
        a = ac[:, :D_EXPERT]
        c = ac[:, D_EXPERT:]
        hmid = (a * jax.nn.sigmoid(a) * c).astype(BF16)
        y = jnp.dot(hmid, wd_bf[...], preferred_element_type=F32)
        _store_packed(ys_ref, _pack_rows(y))


def experts(xs, tile_expert, tile_valid, w_up, w_down, layer_idx, tr=ROW_TILE):
    rows = xs.shape[0] // PACK_ROWS
    nt = rows // tr
    grid_spec = pltpu.PrefetchScalarGridSpec(
        num_scalar_prefetch=2,
        grid=(nt,),
        in_specs=[pl.BlockSpec((tr * PACK_ROWS, LANES), lambda i, te, tv: (i, 0)),
                  pl.BlockSpec((1, 1, D_MODEL, 2 * D_EXPERT), lambda i, te, tv: (layer_idx, te[i], 0, 0)),
                  pl.BlockSpec((1, 1, D_EXPERT, D_MODEL), lambda i, te, tv: (layer_idx, te[i], 0, 0))],
        out_specs=pl.BlockSpec((tr * PACK_ROWS, LANES), lambda i, te, tv: (i, 0)),
        scratch_shapes=[pltpu.VMEM((D_MODEL, 2 * D_EXPERT), BF16), pltpu.VMEM((D_EXPERT, D_MODEL), BF16)],
    )
    return pl.pallas_call(
        _expert_body,
        grid_spec=grid_spec,
        out_shape=_sds((rows * PACK_ROWS, LANES), U32),
        compiler_params=_cparams(("arbitrary",)),
        name="experts",
    )(tile_expert, tile_valid, xs, w_up, w_down)


def _combine_body(seg_ref, imeta_ref, ys_ref, x_ref, meta_ref, o_ref, b1, b2, sem):
    tm = x_ref.shape[0]

    def row_copy(t, k):
        pos = _slot(seg_ref, imeta_ref, k, t)
        src = ys_ref.at[pl.ds(pl.multiple_of(pos * PACK_ROWS, PACK_ROWS), PACK_ROWS), :]
        dst = (b1, b2)[k].at[pl.ds(pl.multiple_of(t * PACK_ROWS, PACK_ROWS), PACK_ROWS), :]
        return pltpu.make_async_copy(src, dst, sem)

    def start(t, carry):
        row_copy(t, 0).start()
        row_copy(t, 1).start()
        return carry

    lax.fori_loop(0, tm, start, 0)
    pltpu.make_async_copy(ys_ref.at[pl.ds(0, tm * PACK_ROWS), :], b1, sem).wait()
    pltpu.make_async_copy(ys_ref.at[pl.ds(0, tm * PACK_ROWS), :], b2, sem).wait()
    meta = meta_ref[...]
    w1 = meta[:, 2:3]
    w2 = meta[:, 3:4]
    half = D_MODEL // 2
    lo1, hi1 = _load_packed(b1, tm)
    lo2, hi2 = _load_packed(b2, tm)
    o_ref[:, :half] = x_ref[:, :half] + w1 * lo1 + w2 * lo2
    o_ref[:, half:] = x_ref[:, half:] + w1 * hi1 + w2 * hi2


def combine(ys, seg_start, imeta, x1, meta):
    t, d = x1.shape
    nt, _, tm = imeta.shape
    grid_spec = pltpu.PrefetchScalarGridSpec(
        num_scalar_prefetch=1,
        grid=(nt,),
        in_specs=[pl.BlockSpec((1, 8, tm), lambda i, seg: (i, 0, 0), memory_space=pltpu.SMEM),
                  pl.BlockSpec(memory_space=pl.ANY),
                  pl.BlockSpec((tm, d), lambda i, seg: (i, 0)),
                  pl.BlockSpec((tm, LANES), lambda i, seg: (i, 0))],
        out_specs=pl.BlockSpec((tm, d), lambda i, seg: (i, 0)),
        scratch_shapes=[pltpu.VMEM((tm * PACK_ROWS, LANES), U32), pltpu.VMEM((tm * PACK_ROWS, LANES), U32),
                        pltpu.SemaphoreType.DMA(())],
    )
    return pl.pallas_call(
        _combine_body,
        grid_spec=grid_spec,
        out_shape=_sds((t, d), F32),
        compiler_params=_cparams(("arbitrary",)),
        name="combine",
    )(seg_start, imeta, ys, x1, meta)


def moe_plan(counts, n_tokens, tr=ROW_TILE):
    n_tiles = (2 * n_tokens) // tr + N_EXPERTS
    cnt = counts[0, :N_EXPERTS].astype(I32)
    tiles = (cnt + tr - 1) // tr
    e = jnp.arange(N_EXPERTS, dtype=I32)
    tile_end = jnp.sum(jnp.where(e[:, None] <= e[None, :], tiles[:, None], 0), axis=0)
    seg_start = (tile_end - tiles) * tr
    n_used = tile_end[N_EXPERTS - 1]
    tid = jnp.arange(n_tiles, dtype=I32)
    tile_valid = (tid < n_used).astype(I32)
    last = jnp.maximum(n_used - 1, 0)
    tile_expert = jnp.sum((tile_end[None, :] <= jnp.minimum(tid, last)[:, None]).astype(I32), axis=1)
    tile_expert = jnp.minimum(tile_expert, N_EXPERTS - 1)
    return seg_start, tile_expert, tile_valid, n_tiles * tr


def mixer_c(z, qn, kn, cos_t, sin_t, batch):
    outs, lses, dils = [], [], []
    for g, (window, dil) in enumerate(DIL_CFG):
        assert window // dil == N_BACK
        o, l = att_c(prep_c(z, g, dil, qn, kn, cos_t, sin_t, batch), dil)
        outs.append(o)
        lses.append(l)
        dils.append(dil)
    return merge_c(outs, lses, dils)


def layer(x, p, lam_init, batch, tabs):
    t, d = x.shape
    s = t // batch
    z, h = proj(x, p["norm_mix"], p["w_in"])
    y_a = mix_a(z, p["sg_norm"], p["sg_w"], p["sg_b"])
    y_b = mix_b(z, p["conv_w"], batch)
    y_c = mixer_c(z, p["qn_c"], p["kn_c"], tabs[0], tabs[1], batch)
    q1, q2, kd, vt = prep_d(z, p["qn_d"], p["kn_d"], tabs[2], tabs[3], s)
    y_d = diff_att(q1.reshape(batch, s, -1), q2.reshape(batch, s, -1), kd.reshape(batch, s, -1), vt,
                   p["lam_q1"], p["lam_k1"], p["lam_q2"], p["lam_k2"], p["subln_d"],
                   lam_init).reshape(t, BRANCH_W)
    merged = gate_merge(h, (y_a, y_b, y_c, y_d), p["w_gate"], p["b_gate"], p["w_branch"])
    x1, hp, meta, counts, imeta = outproj_route(merged, x, p["w_out"], p["norm_ffn"], p["w_r"], p["b_r"])
    seg_start, tile_expert, tile_valid, rows = moe_plan(counts, t)
    xs = dispatch(hp, seg_start, imeta, rows)
    ys = experts(xs, tile_expert, tile_valid, p["w_up"], p["w_down"], p["layer"])
    return combine(ys, seg_start, imeta, x1, meta)


def kernel(x, norm_mix, w_in, sg_norm, sg_w, sg_b, conv_w, qn_c, kn_c, qn_d, kn_d, lam_q1, lam_k1, lam_q2,
           lam_k2, subln_d, w_gate, b_gate, w_branch, w_out, norm_ffn, w_rg, b_rg, w_re, b_re, w_up, w_down):
    batch, seq, d = x.shape
    depth = w_in.shape[0]
    tabs = rope_tables(seq, HEAD_DIM // 2) + rope_tables(seq, DIFF_DIM // 2)
    xf = x.reshape(batch * seq, d)
    for l in range(depth):
        pad = LANES - N_EXPERTS - N_GROUPS
        w_r = jnp.concatenate([w_re[l], w_rg[l], jnp.zeros((d, pad), F32)], axis=1).astype(BF16)
        b_r = jnp.concatenate([b_re[l], b_rg[l], jnp.zeros((pad,), F32)]).reshape(1, LANES)
        p = dict(norm_mix=norm_mix[l], w_in=w_in[l].astype(BF16), sg_norm=sg_norm[l], sg_w=sg_w[l], sg_b=sg_b[l],
                 conv_w=conv_w[l], qn_c=qn_c[l], kn_c=kn_c[l], qn_d=qn_d[l], kn_d=kn_d[l], lam_q1=lam_q1[l],
                 lam_k1=lam_k1[l], lam_q2=lam_q2[l], lam_k2=lam_k2[l], subln_d=subln_d[l],
                 w_gate=w_gate[l].astype(BF16), b_gate=b_gate[l], w_branch=w_branch[l].astype(BF16),
                 w_out=w_out[l].astype(BF16), norm_ffn=norm_ffn[l], w_r=w_r, b_r=b_r,
                 w_up=w_up, w_down=w_down, layer=l)
        xf = layer(xf, p, 0.8 - 0.6 * math.exp(-0.3 * l), batch, tabs)
    return xf.reshape(batch, seq, d)
```

```python
import functools
import math

import jax
import jax.numpy as jnp
from jax import lax
from jax.experimental import pallas as pl
from jax.experimental.pallas import tpu as pltpu

F32 = jnp.float32
BF16 = jnp.bfloat16
I32 = jnp.int32
U32 = jnp.uint32

D_MODEL = 2048
BRANCH_W = 512
HEAD_DIM = 128
N_HEADS = 4
SG_CHUNK = 128
DIL_CFG = ((128, 1), (512, 4), (2048, 16))
N_BACK = 128
DIFF_DIM = 64
ROPE_THETA = 10000.0
W_A = 2 * BRANCH_W
W_B = 3 * BRANCH_W
W_C = 9 * BRANCH_W
N_GROUPS = 4
EXP_PER_GROUP = 8
N_EXPERTS = 32
D_EXPERT = 512
EPS = 1e-6
NEG = -1e30

LANES = 128
PACK_ROWS = D_MODEL // 2 // LANES
VMEM_LIMIT = 56 * 1024 * 1024
ROW_TILE = 512
DIFF_TILE = 256
DIFF_KTILE = 2 * DIFF_TILE
ROUTE_TILE = 512
DMA_UNROLL = 8


def _cparams(sem, **kw):
    return pltpu.CompilerParams(dimension_semantics=sem, vmem_limit_bytes=VMEM_LIMIT, **kw)


def _sds(shape, dtype):
    return jax.ShapeDtypeStruct(shape, dtype)


def _proj_body(x_ref, g_ref, w_ref, z_ref, h_ref):
    @pl.when(pl.program_id(1) == 0)
    def _():
        x = x_ref[...]
        ms = jnp.mean(x * x, axis=-1, keepdims=True)
        h_ref[...] = (x * lax.rsqrt(ms + EPS) * g_ref[...]).astype(BF16)

    z_ref[...] = jnp.dot(h_ref[...], w_ref[...], preferred_element_type=F32).astype(z_ref.dtype)


def proj(x, g, w, tm=1024, tn=512):
    t, d = x.shape
    n = w.shape[1]
    return pl.pallas_call(
        _proj_body,
        grid=(t // tm, n // tn),
        in_specs=[pl.BlockSpec((tm, d), lambda i, j: (i, 0)),
                  pl.BlockSpec((1, d), lambda i, j: (0, 0)),
                  pl.BlockSpec((d, tn), lambda i, j: (0, j))],
        out_specs=[pl.BlockSpec((tm, tn), lambda i, j: (i, j)),
                   pl.BlockSpec((tm, d), lambda i, j: (i, 0))],
        out_shape=[_sds((t, n), BF16), _sds((t, d), BF16)],
        compiler_params=_cparams(("arbitrary", "arbitrary")),
        name="proj",
    )(x, g.reshape(1, d), w)


def _gelu_tanh(x):
    c = math.sqrt(2.0 / math.pi)
    return 0.5 * x * (1.0 + jnp.tanh(c * (x + 0.044715 * (x * x * x))))


def _mixa_body(z_ref, g_ref, w_ref, b_ref, o_ref):
    tm = z_ref.shape[0]
    ga = _gelu_tanh(z_ref[...].astype(F32))
    u = ga[:, :BRANCH_W]
    v = ga[:, BRANCH_W:]
    ms = jnp.mean(v * v, axis=-1, keepdims=True)
    vn = (v * lax.rsqrt(ms + EPS) * g_ref[...]).astype(BF16)
    row = lax.broadcasted_iota(I32, (SG_CHUNK, SG_CHUNK), 0)
    col = lax.broadcasted_iota(I32, (SG_CHUNK, SG_CHUNK), 1)
    for g in range(N_HEADS):
        wg = jnp.where(row >= col, w_ref[g], 0.0).astype(BF16)
        bias = b_ref[g]
        cs = slice(g * SG_CHUNK, (g + 1) * SG_CHUNK)
        for c in range(tm // SG_CHUNK):
            rs = slice(c * SG_CHUNK, (c + 1) * SG_CHUNK)
            mixed = jnp.dot(wg, vn[rs, cs], preferred_element_type=F32) + bias
            o_ref[rs, cs] = (u[rs, cs] * mixed).astype(o_ref.dtype)


def mix_a(z, sg_norm, sg_w, sg_b, tm=512):
    t = z.shape[0]
    bias = jnp.broadcast_to(sg_b[:, :, None], (N_HEADS, SG_CHUNK, SG_CHUNK)).astype(F32)
    return pl.pallas_call(
        _mixa_body,
        grid=(t // tm,),
        in_specs=[pl.BlockSpec((tm, W_A), lambda i: (i, 0)),
                  pl.BlockSpec((1, BRANCH_W), lambda i: (0, 0)),
                  pl.BlockSpec((N_HEADS, SG_CHUNK, SG_CHUNK), lambda i: (0, 0, 0)),
                  pl.BlockSpec((N_HEADS, SG_CHUNK, SG_CHUNK), lambda i: (0, 0, 0))],
        out_specs=pl.BlockSpec((tm, BRANCH_W), lambda i: (i, 0)),
        out_shape=_sds((t, BRANCH_W), BF16),
        compiler_params=_cparams(("arbitrary",)),
        name="mix_a",
    )(z, sg_norm.reshape(1, BRANCH_W), sg_w, bias)


def _mixb_body(bg_ref, cg_ref, hb_ref, w_ref, o_ref, prev_ref):
    @pl.when(pl.program_id(1) == 0)
    def _():
        prev_ref[...] = jnp.zeros_like(prev_ref)

    zz = cg_ref[...].astype(F32) * hb_ref[...].astype(F32)
    tm = zz.shape[0]
    row = lax.broadcasted_iota(I32, zz.shape, 0)
    p = prev_ref[...]
    z1 = jnp.where(row == 0, p[7:8, :], pltpu.roll(zz, 1, 0))
    z2 = jnp.where(row == 0, p[6:7, :], jnp.where(row == 1, p[7:8, :], pltpu.roll(zz, 2, 0)))
    w = w_ref[...]
    y = w[0:1, :] * z2 + w[1:2, :] * z1 + w[2:3, :] * zz
    o_ref[...] = (bg_ref[...].astype(F32) * y).astype(o_ref.dtype)
    prev_ref[...] = zz[tm - 8:, :]


def mix_b(z, conv_w, batch, tm=512):
    t = z.shape[0]
    nt = t // batch // tm
    c0 = W_A // BRANCH_W

    def zspec(k):
        return pl.BlockSpec((tm, BRANCH_W), lambda b, i: (b * nt + i, c0 + k))

    return pl.pallas_call(
        _mixb_body,
        grid=(batch, nt),
        in_specs=[zspec(0), zspec(1), zspec(2), pl.BlockSpec((3, BRANCH_W), lambda b, i: (0, 0))],
        out_specs=pl.BlockSpec((tm, BRANCH_W), lambda b, i: (b * nt + i, 0)),
        out_shape=_sds((t, BRANCH_W), BF16),
        scratch_shapes=[pltpu.VMEM((8, BRANCH_W), F32)],
        compiler_params=_cparams(("arbitrary", "arbitrary")),
        name="mix_b",
    )(z, z, z, conv_w)


def rope_tables(s, half):
    inv = ROPE_THETA ** (-jnp.arange(half, dtype=F32) / half)
    ang = jnp.arange(s, dtype=F32)[:, None] * inv[None, :]
    cos, sin = jnp.cos(ang), jnp.sin(ang)
    reps = LANES // (2 * half)
    cos_t = jnp.tile(jnp.concatenate([cos, cos], axis=-1), (1, reps))
    sin_t = jnp.tile(jnp.concatenate([-sin, sin], axis=-1), (1, reps))
    return cos_t, sin_t


def _prep_c_body(dil, zq_ref, zk_ref, zv_ref, gq_ref, gk_ref, cos_ref, sin_ref, o_ref, scr_ref):
    tm = zq_ref.shape[0]
    cos = cos_ref[...]
    sin = sin_ref[...]

    def norm_rope(xh, gain):
        ms = jnp.mean(xh * xh, axis=-1, keepdims=True)
        xn = xh * lax.rsqrt(ms + EPS) * gain
        return xn * cos + pltpu.roll(xn, HEAD_DIM // 2, 1) * sin

    width = 3 * BRANCH_W
    for w, ref in enumerate((zq_ref, zk_ref, zv_ref)):
        for h in range(N_HEADS):
            xh = ref[:, h * HEAD_DIM:(h + 1) * HEAD_DIM].astype(F32)
            val = xh if w == 2 else norm_rope(xh, (gq_ref, gk_ref)[w][...])
            c = w * BRANCH_W + h * HEAD_DIM
            if dil == 1:
                o_ref[0, :, c:c + HEAD_DIM] = val.astype(o_ref.dtype)
            else:
                buf = scr_ref.at[w * N_HEADS + h]
                buf[...] = val
                for r in range(dil):
                    o_ref[0, :, r * width + c:r * width + c + HEAD_DIM] = (
                        buf[pl.ds(r, tm // dil, stride=dil), :].astype(o_ref.dtype))


def prep_c(z, g, dil, qn, kn, cos_t, sin_t, batch, tm=512):
    t = z.shape[0]
    seq = t // batch
    ns = seq // tm
    c0 = (W_A + W_B) // BRANCH_W + g
    gq = (qn * (HEAD_DIM ** -0.5 * math.log2(math.e))).reshape(1, HEAD_DIM).astype(F32)
    gk = kn.reshape(1, HEAD_DIM).astype(F32)
    vec = pl.BlockSpec((1, HEAD_DIM), lambda i: (0, 0))
    tab = pl.BlockSpec((tm, LANES), lambda i: (i % ns, 0))
    return pl.pallas_call(
        functools.partial(_prep_c_body, dil),
        grid=(t // tm,),
        in_specs=[pl.BlockSpec((tm, BRANCH_W), lambda i: (i, c0)),
                  pl.BlockSpec((tm, BRANCH_W), lambda i: (i, c0 + 3)),
                  pl.BlockSpec((tm, BRANCH_W), lambda i: (i, c0 + 6)), vec, vec, tab, tab],
        out_specs=pl.BlockSpec((1, tm // dil, dil * 3 * BRANCH_W), lambda i: (i // ns, i % ns, 0)),
        out_shape=_sds((batch, seq // dil, dil * 3 * BRANCH_W), BF16),
        scratch_shapes=[pltpu.VMEM((3 * N_HEADS, tm, HEAD_DIM), F32)],
        compiler_params=_cparams(("arbitrary",)),
        name="prep_c",
    )(z, z, z, gq, gk, cos_t, sin_t)


def _attc_body(q_ref, kc_ref, kp_ref, vc_ref, vp_ref, o_ref, lse_ref):
    i = pl.program_id(1)
    nsub = q_ref.shape[1] // N_BACK
    qi = lax.broadcasted_iota(I32, (N_BACK, N_BACK), 0)
    kj = lax.broadcasted_iota(I32, (N_BACK, N_BACK), 1)
    ok_own = kj <= qi
    ok_before = kj >= qi
    ok_first = (kj + jnp.where(i > 0, 0, -2 * N_BACK)) >= qi
    dn = (((1,), (1,)), ((), ()))
    heads = [slice(h * HEAD_DIM, (h + 1) * HEAD_DIM) for h in range(N_HEADS)]
    subs = [slice(u * N_BACK, (u + 1) * N_BACK) for u in range(nsub)]
    scores = []
    for h in heads:
        for u in range(nsub):
            q = q_ref[0, subs[u], h]
            k_before = kp_ref[0, :, h] if u == 0 else kc_ref[0, subs[u - 1], h]
            s_before = lax.dot_general(q, k_before, dn, preferred_element_type=F32)
            s_own = lax.dot_general(q, kc_ref[0, subs[u], h], dn, preferred_element_type=F32)
            scores.append((jnp.where(ok_first if u == 0 else ok_before, s_before, NEG),
                           jnp.where(ok_own, s_own, NEG)))
    probs = []
    for s_before, s_own in scores:
        m = jnp.maximum(jnp.max(s_before, axis=-1, keepdims=True), jnp.max(s_own, axis=-1, keepdims=True))
        p_before = jnp.exp2(s_before - m)
        p_own = jnp.exp2(s_own - m)
        l = jnp.sum(p_before, axis=-1, keepdims=True) + jnp.sum(p_own, axis=-1, keepdims=True)
        probs.append((p_before.astype(BF16), p_own.astype(BF16), m, l))
    n = 0
    for h in heads:
        for u in range(nsub):
            p_before, p_own, m, l = probs[n]
            n += 1
            v_before = vp_ref[0, :, h] if u == 0 else vc_ref[0, subs[u - 1], h]
            o = (jnp.dot(p_before, v_before, preferred_element_type=F32)
                 + jnp.dot(p_own, vc_ref[0, subs[u], h], preferred_element_type=F32))
            o_ref[0, subs[u], h] = (o * (1.0 / l)).astype(o_ref.dtype)
            lse_ref[0, subs[u], h] = jnp.broadcast_to(m + jnp.log2(l), (N_BACK, HEAD_DIM))


def att_c(qkv, dil, tq=256):
    b, length, _ = qkv.shape
    r = tq // N_BACK

    def cur(w):
        return pl.BlockSpec((1, tq, BRANCH_W), lambda n, i: (n // dil, i, (n % dil) * 3 + w))

    def prev(w):
        return pl.BlockSpec((1, N_BACK, BRANCH_W),
                            lambda n, i: (n // dil, jnp.maximum(i * r - 1, 0), (n % dil) * 3 + w))

    out = pl.BlockSpec((1, tq, BRANCH_W), lambda n, i: (n // dil, i, n % dil))
    return pl.pallas_call(
        _attc_body,
        grid=(b * dil, length // tq),
        in_specs=[cur(0), cur(1), prev(1), cur(2), prev(2)],
        out_specs=[out, out],
        out_shape=[_sds((b, length, dil * BRANCH_W), BF16), _sds((b, length, dil * BRANCH_W), F32)],
        compiler_params=_cparams(("arbitrary", "arbitrary")),
        name="att_c",
    )(qkv, qkv, qkv, qkv, qkv)


def _merge_c_body(dils, *refs):
    n = len(dils)
    o_refs, l_refs, y_ref, scr = refs[:n], refs[n:2 * n], refs[2 * n], refs[2 * n + 1:]
    tm = y_ref.shape[0]

    def natural(ref, dil, buf):
        if dil == 1:
            return ref[0].astype(F32)
        for h in range(N_HEADS):
            for r in range(dil):
                c = r * BRANCH_W + h * HEAD_DIM
                buf[h, pl.ds(r, tm // dil, stride=dil), :] = ref[0, :, c:c + HEAD_DIM].astype(F32)
        return jnp.concatenate([buf[h] for h in range(N_HEADS)], axis=-1)

    outs = [natural(o_refs[g], dils[g], scr[2 * g]) for g in range(n)]
    lses = [natural(l_refs[g], dils[g], scr[2 * g + 1]) for g in range(n)]
    m = functools.reduce(jnp.maximum, lses)
    es = [jnp.exp2(l - m) for l in lses]
    y = functools.reduce(lambda a, b: a + b, [e * o for e, o in zip(es, outs)])
    y_ref[...] = (y / functools.reduce(lambda a, b: a + b, es)).astype(y_ref.dtype)


def merge_c(outs, lses, dils, tm=512):
    batch, seq = outs[0].shape[0], outs[0].shape[1] * dils[0]
    t = batch * seq
    ns = seq // tm
    specs = [pl.BlockSpec((1, tm // d, d * BRANCH_W), lambda i: (i // ns, i % ns, 0)) for d in dils]
    return pl.pallas_call(
        functools.partial(_merge_c_body, tuple(dils)),
        grid=(t // tm,),
        in_specs=specs + specs,
        out_specs=pl.BlockSpec((tm, BRANCH_W), lambda i: (i, 0)),
        out_shape=_sds((t, BRANCH_W), BF16),
        scratch_shapes=[pltpu.VMEM((N_HEADS, tm, HEAD_DIM), F32) for _ in range(2 * len(dils))],
        compiler_params=_cparams(("arbitrary",)),
        name="merge_c",
    )(*outs, *lses)


def _norm_rope_half(xh, gain, cos, sin, lane):
    sq = xh * xh
    lo = lane < DIFF_DIM
    s0 = jnp.sum(jnp.where(lo, sq, 0.0), axis=-1, keepdims=True)
    s1 = jnp.sum(jnp.where(lo, 0.0, sq), axis=-1, keepdims=True)
    ms = jnp.where(lo, s0, s1) * (1.0 / DIFF_DIM)
    xn = xh * lax.rsqrt(ms + EPS) * gain
    first = (lane % DIFF_DIM) < (DIFF_DIM // 2)
    rot = jnp.where(first, pltpu.roll(xn, LANES - DIFF_DIM // 2, 1), pltpu.roll(xn, DIFF_DIM // 2, 1))
    return xn * cos + rot * sin


def _prep_d_body(zq_ref, zk_ref, zv_ref, gq_ref, gk_ref, cos_ref, sin_ref, q1_ref, q2_ref, k_ref, vt_ref):
    tm = zq_ref.shape[0]
    lane = lax.broadcasted_iota(I32, (tm, LANES), 1)
    lo = lane < DIFF_DIM
    cos = cos_ref[...]
    sin = sin_ref[...]
    zq = zq_ref[...].astype(F32)
    zk = zk_ref[...].astype(F32)
    for h in range(N_HEADS):
        sl = slice(h * HEAD_DIM, (h + 1) * HEAD_DIM)
        q = _norm_rope_half(zq[:, sl], gq_ref[...], cos, sin, lane)
        q1_ref[:, sl] = jnp.where(lo, q, 0.0).astype(BF16)
        q2_ref[:, sl] = jnp.where(lo, 0.0, q).astype(BF16)
        k_ref[:, sl] = _norm_rope_half(zk[:, sl], gk_ref[...], cos, sin, lane).astype(BF16)
    vt_ref[0] = zv_ref[...].astype(F32).T.astype(BF16)


def prep_d(z, qn, kn, cos_t, sin_t, seq, tm=DIFF_KTILE):
    t = z.shape[0]
    ns = seq // tm
    c0 = (W_A + W_B + W_C) // BRANCH_W
    gq = jnp.tile(qn * (DIFF_DIM ** -0.5 * math.log2(math.e)), 2).reshape(1, LANES).astype(F32)
    gk = jnp.tile(kn, 2).reshape(1, LANES).astype(F32)
    row = pl.BlockSpec((tm, BRANCH_W), lambda i: (i, 0))
    tab = pl.BlockSpec((tm, LANES), lambda i: (i % ns, 0))
    vec = pl.BlockSpec((1, LANES), lambda i: (0, 0))
    return pl.pallas_call(
        _prep_d_body,
        grid=(t // tm,),
        in_specs=[pl.BlockSpec((tm, BRANCH_W), lambda i: (i, c0)),
                  pl.BlockSpec((tm, BRANCH_W), lambda i: (i, c0 + 1)),
                  pl.BlockSpec((tm, BRANCH_W), lambda i: (i, c0 + 2)), vec, vec, tab, tab],
        out_specs=[row, row, row, pl.BlockSpec((1, BRANCH_W, tm), lambda i: (i, 0, 0))],
        out_shape=[_sds((t, BRANCH_W), BF16)] * 3 + [_sds((t // tm, BRANCH_W, tm), BF16)],
        compiler_params=_cparams(("arbitrary",)),
        name="prep_d",
    )(z, z, z, gq, gk, cos_t, sin_t)


def _diff_body(lam_init, q1_ref, q2_ref, k_ref, vt_ref, lq1, lk1, lq2, lk2, sg_ref, o_ref, acc_ref):
    i = pl.program_id(1)
    tq = q1_ref.shape[1]
    n_chain = 2 * N_HEADS
    lam = (jnp.exp(jnp.sum(lq1[...] * lk1[...], axis=-1, keepdims=True))
           - jnp.exp(jnp.sum(lq2[...] * lk2[...], axis=-1, keepdims=True)) + lam_init)
    tk = vt_ref.shape[2]
    dn = (((1,), (1,)), ((), ()))
    acc_ref[...] = jnp.zeros_like(acc_ref)
    heads = [slice(h * HEAD_DIM, (h + 1) * HEAD_DIM) for h in range(N_HEADS)]

    def step(j, carry, n_keys, shift):
        ms, ls = carry
        start = pl.multiple_of(j * tk, tk)
        scores = []
        for c in range(n_chain):
            q = (q1_ref, q2_ref)[c % 2][0, :, heads[c // 2]]
            scores.append(lax.dot_general(k_ref[0, pl.ds(start, n_keys), heads[c // 2]], q, dn,
                                          preferred_element_type=F32))
        if shift is not None:
            krow = lax.broadcasted_iota(I32, (n_keys, tq), 0)
            qcol = lax.broadcasted_iota(I32, (n_keys, tq), 1)
            visible = krow <= qcol + shift
        new_m, new_l, alphas, probs = [], [], [], []
        for c in range(n_chain):
            s = scores[c] if shift is None else jnp.where(visible, scores[c], NEG)
            m_new = jnp.maximum(ms[c], jnp.max(s, axis=0, keepdims=True))
            a = jnp.exp2(ms[c] - m_new)
            p = jnp.exp2(s - m_new)
            new_l.append(a * ls[c] + jnp.sum(p, axis=0, keepdims=True))
            new_m.append(m_new)
            alphas.append(a)
            probs.append(p.astype(BF16))
        for c in range(n_chain):
            pv = jnp.dot(vt_ref[j, heads[c // 2], 0:n_keys], probs[c], preferred_element_type=F32)
            acc_ref[c] = alphas[c] * acc_ref[c] + pv
        return tuple(new_m), tuple(new_l)

    init = (tuple(jnp.full((1, tq), NEG, F32) for _ in range(n_chain)),
            tuple(jnp.zeros((1, tq), F32) for _ in range(n_chain)))
    n_full = i // 2
    carry = lax.fori_loop(0, n_full, functools.partial(step, n_keys=tk, shift=None), init)
    _, ls = lax.cond(i % 2 == 1,
                     lambda c: step(n_full, c, tk, tq),
                     lambda c: step(n_full, c, tq, 0), carry)
    for h in range(N_HEADS):
        o1 = acc_ref[2 * h] * (1.0 / ls[2 * h])
        o2 = acc_ref[2 * h + 1] * (1.0 / ls[2 * h + 1])
        a = o1 - lam * o2
        ms = jnp.mean(a * a, axis=0, keepdims=True)
        y = a * lax.rsqrt(ms + EPS) * sg_ref[...]
        o_ref[0, :, h * HEAD_DIM:(h + 1) * HEAD_DIM] = y.T.astype(o_ref.dtype)


def diff_att(q1, q2, k, vt, lq1, lk1, lq2, lk2, sub_g, lam_init, tq=DIFF_TILE):
    b, s, _ = q1.shape
    tk = vt.shape[2]
    assert tk == 2 * tq
    nk = s // tq
    qs = pl.BlockSpec((1, tq, BRANCH_W), lambda n, i: (n, i, 0))
    vec = pl.BlockSpec((1, DIFF_DIM), lambda n, i: (0, 0))
    sg = jnp.broadcast_to((sub_g * (1.0 - lam_init))[:, None], (HEAD_DIM, tq)).astype(F32)
    return pl.pallas_call(
        functools.partial(_diff_body, lam_init),
        grid=(b, nk),
        in_specs=[qs, qs,
                  pl.BlockSpec((1, s, BRANCH_W), lambda n, i: (n, 0, 0)),
                  pl.BlockSpec((s // tk, BRANCH_W, tk), lambda n, i: (n, 0, 0)),
                  vec, vec, vec, vec,
                  pl.BlockSpec((HEAD_DIM, tq), lambda n, i: (0, 0))],
        out_specs=qs,
        out_shape=_sds((b, s, BRANCH_W), BF16),
        scratch_shapes=[pltpu.VMEM((2 * N_HEADS, HEAD_DIM, tq), F32)],
        compiler_params=_cparams(("arbitrary", "arbitrary")),
        name="diff_att",
    )(q1, q2, k, vt, lq1.reshape(1, -1), lk1.reshape(1, -1), lq2.reshape(1, -1), lk2.reshape(1, -1), sg)


def _gate_body(h_ref, ya, yb, yc, yd, wg_ref, bg_ref, wb_ref, o_ref):
    h = h_ref[...]
    acc = None
    for i, y in enumerate((ya, yb, yc, yd)):
        g = jax.nn.sigmoid(jnp.dot(h, wg_ref[i], preferred_element_type=F32) + bg_ref[i:i + 1, :])
        t = jnp.dot(y[...], wb_ref[i], preferred_element_type=F32)
        acc = g * t if acc is None else acc + g * t
    o_ref[...] = acc.astype(o_ref.dtype)


def gate_merge(h, ys, w_gate, b_gate, w_branch, tm=512, tn=512):
    t, d = h.shape
    ysp = pl.BlockSpec((tm, BRANCH_W), lambda j, i: (i, 0))
    return pl.pallas_call(
        _gate_body,
        grid=(d // tn, t // tm),
        in_specs=[pl.BlockSpec((tm, d), lambda j, i: (i, 0)), ysp, ysp, ysp, ysp,
                  pl.BlockSpec((4, d, tn), lambda j, i: (0, 0, j)),
                  pl.BlockSpec((4, tn), lambda j, i: (0, j)),
                  pl.BlockSpec((4, BRANCH_W, tn), lambda j, i: (0, 0, j))],
        out_specs=pl.BlockSpec((tm, tn), lambda j, i: (i, j)),
        out_shape=_sds((t, d), BF16),
        compiler_params=_cparams(("arbitrary", "arbitrary")),
        name="gate_merge",
    )(h, *ys, w_gate, b_gate, w_branch)


def _pack_rows(x):
    half = x.shape[1] // 2
    lo = pltpu.bitcast(x[:, :half].astype(BF16).astype(F32), U32)
    hi = pltpu.bitcast(x[:, half:].astype(BF16).astype(F32), U32)
    return (lo >> 16) | hi


def _store_packed(ref, words):
    m = words.shape[0]
    for c in range(PACK_ROWS):
        ref[pl.ds(c, m, stride=PACK_ROWS), :] = words[:, c * LANES:(c + 1) * LANES]


def _load_packed(ref, m):
    words = jnp.concatenate([ref[pl.ds(c, m, stride=PACK_ROWS), :] for c in range(PACK_ROWS)], axis=-1)
    lo = pltpu.bitcast(words << 16, F32)
    hi = pltpu.bitcast(words & jnp.uint32(0xFFFF0000), F32)
    return lo, hi


def _outproj_body(m_ref, x_ref, wo_ref, g_ref, wr_ref, br_ref, x1_ref, hp_ref, meta_ref, cnt_ref, imeta_ref,
                  run_ref):
    @pl.when(pl.program_id(0) == 0)
    def _():
        run_ref[...] = jnp.zeros_like(run_ref)

    tm = x_ref.shape[0]
    x1 = x_ref[...] + jnp.dot(m_ref[...], wo_ref[...], preferred_element_type=F32)
    x1_ref[...] = x1
    ms = jnp.mean(x1 * x1, axis=-1, keepdims=True)
    h2 = x1 * lax.rsqrt(ms + EPS) * g_ref[...]
    _store_packed(hp_ref, _pack_rows(h2))

    lg = jnp.dot(h2.astype(BF16), wr_ref[...], preferred_element_type=F32) + br_ref[...]
    lane = lax.broadcasted_iota(I32, (tm, LANES), 1)
    big = jnp.int32(4 * LANES)
    is_g = (lane >= N_EXPERTS) & (lane < N_EXPERTS + N_GROUPS)
    gl = jnp.where(is_g, lg, NEG)
    gm = jnp.max(gl, axis=-1, keepdims=True)
    g_p = 1.0 / jnp.sum(jnp.exp(gl - gm), axis=-1, keepdims=True)
    gidx = jnp.min(jnp.where(gl == gm, lane, big), axis=-1, keepdims=True) - N_EXPERTS
    lo_e = gidx * EXP_PER_GROUP
    in_g = (lane >= lo_e) & (lane < lo_e + EXP_PER_GROUP)
    el = jnp.where(in_g, lg, NEG)
    v1 = jnp.max(el, axis=-1, keepdims=True)
    i1 = jnp.min(jnp.where(el == v1, lane, big), axis=-1, keepdims=True)
    el2 = jnp.where(lane == i1, NEG, el)
    v2 = jnp.max(el2, axis=-1, keepdims=True)
    i2 = jnp.min(jnp.where(el2 == v2, lane, big), axis=-1, keepdims=True)
    e2 = jnp.exp(v2 - v1)
    w1 = g_p / (1.0 + e2)
    w2 = g_p * e2 / (1.0 + e2)

    oh1 = lane == i1
    oh2 = lane == i2
    a = jnp.where(oh1, 1.0, jnp.where(oh2, 1.0, 0.0))
    r = lax.broadcasted_iota(I32, (tm, tm), 0)
    c = lax.broadcasted_iota(I32, (tm, tm), 1)
    before = jnp.where(c < r, 1.0, 0.0).astype(BF16)
    pref = jnp.dot(before, a.astype(BF16), preferred_element_type=F32) + run_ref[0:1, :]
    rank1 = jnp.sum(jnp.where(oh1, pref, 0.0), axis=-1, keepdims=True)
    rank2 = jnp.sum(jnp.where(oh2, pref, 0.0), axis=-1, keepdims=True)
    new_run = run_ref[0:1, :] + jnp.sum(a, axis=0, keepdims=True)
    run_ref[...] = jnp.broadcast_to(new_run, run_ref.shape)
    cnt_ref[...] = jnp.broadcast_to(new_run, cnt_ref.shape)

    meta = jnp.where(lane == 0, i1.astype(F32), 0.0)
    meta = jnp.where(lane == 1, i2.astype(F32), meta)
    meta = jnp.where(lane == 2, w1, meta)
    meta = jnp.where(lane == 3, w2, meta)
    meta = jnp.where(lane == 4, rank1, meta)
    meta = jnp.where(lane == 5, rank2, meta)
    meta_ref[...] = meta
    imeta_ref[0] = meta.T[0:8, :].astype(I32)


def outproj_route(merged, x, w_out, g2, w_r, b_r, tm=ROUTE_TILE):
    t, d = x.shape
    return pl.pallas_call(
        _outproj_body,
        grid=(t // tm,),
        in_specs=[pl.BlockSpec((tm, d), lambda i: (i, 0)),
                  pl.BlockSpec((tm, d), lambda i: (i, 0)),
                  pl.BlockSpec((d, d), lambda i: (0, 0)),
                  pl.BlockSpec((1, d), lambda i: (0, 0)),
                  pl.BlockSpec((d, LANES), lambda i: (0, 0)),
                  pl.BlockSpec((1, LANES), lambda i: (0, 0))],
        out_specs=[pl.BlockSpec((tm, d), lambda i: (i, 0)),
                   pl.BlockSpec((tm * PACK_ROWS, LANES), lambda i: (i, 0)),
                   pl.BlockSpec((tm, LANES), lambda i: (i, 0)),
                   pl.BlockSpec((8, LANES), lambda i: (0, 0)),
                   pl.BlockSpec((1, 8, tm), lambda i: (i, 0, 0))],
        out_shape=[_sds((t, d), F32), _sds((t * PACK_ROWS, LANES), U32), _sds((t, LANES), F32),
                   _sds((8, LANES), F32), _sds((t // tm, 8, tm), I32)],
        scratch_shapes=[pltpu.VMEM((8, LANES), F32)],
        compiler_params=_cparams(("arbitrary",)),
        name="outproj_route",
    )(merged, x, w_out, g2.reshape(1, d), w_r, b_r)


def _slot(seg_ref, imeta_ref, k, t):
    return seg_ref[imeta_ref[0, k, t]] + imeta_ref[0, 4 + k, t]


def _dispatch_body(seg_ref, imeta_ref, hp_ref, xs_ref, sem):
    tm = hp_ref.shape[0] // PACK_ROWS

    def row_copy(t, k):
        pos = _slot(seg_ref, imeta_ref, k, t)
        src = hp_ref.at[pl.ds(pl.multiple_of(t * PACK_ROWS, PACK_ROWS), PACK_ROWS), :]
        dst = xs_ref.at[pl.ds(pl.multiple_of(pos * PACK_ROWS, PACK_ROWS), PACK_ROWS), :]
        return pltpu.make_async_copy(src, dst, sem)

    def start(t, carry):
        row_copy(t, 0).start()
        row_copy(t, 1).start()
        return carry

    lax.fori_loop(0, tm, start, 0, unroll=DMA_UNROLL)
    whole = pltpu.make_async_copy(hp_ref, xs_ref.at[pl.ds(0, tm * PACK_ROWS), :], sem)
    whole.wait()
    whole.wait()


def dispatch(hp, seg_start, imeta, rows):
    nt, _, tm = imeta.shape
    grid_spec = pltpu.PrefetchScalarGridSpec(
        num_scalar_prefetch=1,
        grid=(nt,),
        in_specs=[pl.BlockSpec((1, 8, tm), lambda i, seg: (i, 0, 0), memory_space=pltpu.SMEM),
                  pl.BlockSpec((tm * PACK_ROWS, LANES), lambda i, seg: (i, 0))],
        out_specs=pl.BlockSpec(memory_space=pl.ANY),
        scratch_shapes=[pltpu.SemaphoreType.DMA(())],
    )
    return pl.pallas_call(
        _dispatch_body,
        grid_spec=grid_spec,
        out_shape=_sds((rows * PACK_ROWS, LANES), U32),
        compiler_params=_cparams(("arbitrary",), has_side_effects=True),
        name="dispatch",
    )(seg_start, imeta, hp)


def _expert_body(te_ref, tv_ref, xs_ref, wu_ref, wd_ref, ys_ref, wu_bf, wd_bf):
    i = pl.program_id(0)
    tr = xs_ref.shape[0] // PACK_ROWS

    @pl.when(jnp.logical_or(i == 0, te_ref[i] != te_ref[jnp.maximum(i - 1, 0)]))
    def _():
        wu_bf[...] = wu_ref[0, 0].astype(BF16)
        wd_bf[...] = wd_ref[0, 0].astype(BF16)

    @pl.when(tv_ref[i] > 0)
    def _():
        lo, hi = _load_packed(xs_ref, tr)
        x = jnp.concatenate([lo, hi], axis=-1).astype(BF16)
        ac = jnp.dot(x, wu_bf[...], preferred_element_type=F32)
        a = ac[:, :D_EXPERT]
        c = ac[:, D_EXPERT:]
        hmid = (a * jax.nn.sigmoid(a) * c).astype(BF16)
        y = jnp.dot(hmid, wd_bf[...], preferred_element_type=F32)
        _store_packed(ys_ref, _pack_rows(y))


def experts(xs, tile_expert, tile_valid, w_up, w_down, layer_idx, tr=ROW_TILE):
    rows = xs.shape[0] // PACK_ROWS
    nt = rows // tr
    grid_spec = pltpu.PrefetchScalarGridSpec(
        num_scalar_prefetch=2,
        grid=(nt,),
        in_specs=[pl.BlockSpec((tr * PACK_ROWS, LANES), lambda i, te, tv: (i, 0)),
                  pl.BlockSpec((1, 1, D_MODEL, 2 * D_EXPERT), lambda i, te, tv: (layer_idx, te[i], 0, 0)),
                  pl.BlockSpec((1, 1, D_EXPERT, D_MODEL), lambda i, te, tv: (layer_idx, te[i], 0, 0))],
        out_specs=pl.BlockSpec((tr * PACK_ROWS, LANES), lambda i, te, tv: (i, 0)),
        scratch_shapes=[pltpu.VMEM((D_MODEL, 2 * D_EXPERT), BF16), pltpu.VMEM((D_EXPERT, D_MODEL), BF16)],
    )
    return pl.pallas_call(
        _expert_body,
        grid_spec=grid_spec,
        out_shape=_sds((rows * PACK_ROWS, LANES), U32),
        compiler_params=_cparams(("arbitrary",)),
        name="experts",
    )(tile_expert, tile_valid, xs, w_up, w_down)


def _combine_body(seg_ref, imeta_ref, ys_ref, x_ref, meta_ref, o_ref, b1, b2, sem):
    tm = x_ref.shape[0]

    def row_copy(t, k):
        pos = _slot(seg_ref, imeta_ref, k, t)
        src = ys_ref.at[pl.ds(pl.multiple_of(pos * PACK_ROWS, PACK_ROWS), PACK_ROWS), :]
        dst = (b1, b2)[k].at[pl.ds(pl.multiple_of(t * PACK_ROWS, PACK_ROWS), PACK_ROWS), :]
        return pltpu.make_async_copy(src, dst, sem)

    def start(t, carry):
        row_copy(t, 0).start()
        row_copy(t, 1).start()
        return carry

    lax.fori_loop(0, tm, start, 0, unroll=DMA_UNROLL)
    pltpu.make_async_copy(ys_ref.at[pl.ds(0, tm * PACK_ROWS), :], b1, sem).wait()
    pltpu.make_async_copy(ys_ref.at[pl.ds(0, tm * PACK_ROWS), :], b2, sem).wait()
    meta = meta_ref[...]
    w1 = meta[:, 2:3]
    w2 = meta[:, 3:4]
    half = D_MODEL // 2
    lo1, hi1 = _load_packed(b1, tm)
    lo2, hi2 = _load_packed(b2, tm)
    o_ref[:, :half] = x_ref[:, :half] + w1 * lo1 + w2 * lo2
    o_ref[:, half:] = x_ref[:, half:] + w1 * hi1 + w2 * hi2


def combine(ys, seg_start, imeta, x1, meta):
    t, d = x1.shape
    nt, _, tm = imeta.shape
    grid_spec = pltpu.PrefetchScalarGridSpec(
        num_scalar_prefetch=1,
        grid=(nt,),
        in_specs=[pl.BlockSpec((1, 8, tm), lambda i, seg: (i, 0, 0), memory_space=pltpu.SMEM),
                  pl.BlockSpec(memory_space=pl.ANY),
                  pl.BlockSpec((tm, d), lambda i, seg: (i, 0)),
                  pl.BlockSpec((tm, LANES), lambda i, seg: (i, 0))],
        out_specs=pl.BlockSpec((tm, d), lambda i, seg: (i, 0)),
        scratch_shapes=[pltpu.VMEM((tm * PACK_ROWS, LANES), U32), pltpu.VMEM((tm * PACK_ROWS, LANES), U32),
                        pltpu.SemaphoreType.DMA(())],
    )
    return pl.pallas_call(
        _combine_body,
        grid_spec=grid_spec,
        out_shape=_sds((t, d), F32),
        compiler_params=_cparams(("arbitrary",)),
        name="combine",
    )(seg_start, imeta, ys, x1, meta)


def moe_plan(counts, n_tokens, tr=ROW_TILE):
    n_tiles = (2 * n_tokens) // tr + N_EXPERTS
    cnt = counts[0, :N_EXPERTS].astype(I32)
    tiles = (cnt + tr - 1) // tr
    e = jnp.arange(N_EXPERTS, dtype=I32)
    tile_end = jnp.sum(jnp.where(e[:, None] <= e[None, :], tiles[:, None], 0), axis=0)
    seg_start = (tile_end - tiles) * tr
    n_used = tile_end[N_EXPERTS - 1]
    tid = jnp.arange(n_tiles, dtype=I32)
    tile_valid = (tid < n_used).astype(I32)
    last = jnp.maximum(n_used - 1, 0)
    tile_expert = jnp.sum((tile_end[None, :] <= jnp.minimum(tid, last)[:, None]).astype(I32), axis=1)
    tile_expert = jnp.minimum(tile_expert, N_EXPERTS - 1)
    return seg_start, tile_expert, tile_valid, n_tiles * tr


def mixer_c(z, qn, kn, cos_t, sin_t, batch):
    outs, lses, dils = [], [], []
    for g, (window, dil) in enumerate(DIL_CFG):
        assert window // dil == N_BACK
        o, l = att_c(prep_c(z, g, dil, qn, kn, cos_t, sin_t, batch), dil)
        outs.append(o)
        lses.append(l)
        dils.append(dil)
    return merge_c(outs, lses, dils)


def layer(x, p, lam_init, batch, tabs):
    t, d = x.shape
    s = t // batch
    z, h = proj(x, p["norm_mix"], p["w_in"])
    y_a = mix_a(z, p["sg_norm"], p["sg_w"], p["sg_b"])
    y_b = mix_b(z, p["conv_w"], batch)
    y_c = mixer_c(z, p["qn_c"], p["kn_c"], tabs[0], tabs[1], batch)
    q1, q2, kd, vt = prep_d(z, p["qn_d"], p["kn_d"], tabs[2], tabs[3], s)
    y_d = diff_att(q1.reshape(batch, s, -1), q2.reshape(batch, s, -1), kd.reshape(batch, s, -1), vt,
                   p["lam_q1"], p["lam_k1"], p["lam_q2"], p["lam_k2"], p["subln_d"],
                   lam_init).reshape(t, BRANCH_W)
    merged = gate_merge(h, (y_a, y_b, y_c, y_d), p["w_gate"], p["b_gate"], p["w_branch"])
    x1, hp, meta, counts, imeta = outproj_route(merged, x, p["w_out"], p["norm_ffn"], p["w_r"], p["b_r"])
    seg_start, tile_expert, tile_valid, rows = moe_plan(counts, t)
    xs = dispatch(hp, seg_start, imeta, rows)
    ys = experts(xs, tile_expert, tile_valid, p["w_up"], p["w_down"], p["layer"])
    return combine(ys, seg_start, imeta, x1, meta)


def kernel(x, norm_mix, w_in, sg_norm, sg_w, sg_b, conv_w, qn_c, kn_c, qn_d, kn_d, lam_q1, lam_k1, lam_q2,
           lam_k2, subln_d, w_gate, b_gate, w_branch, w_out, norm_ffn, w_rg, b_rg, w_re, b_re, w_up, w_down):
    batch, seq, d = x.shape
    depth = w_in.shape[0]
    tabs = rope_tables(seq, HEAD_DIM // 2) + rope_tables(seq, DIFF_DIM // 2)
    xf = x.reshape(batch * seq, d)
    for l in range(depth):
        pad = LANES - N_EXPERTS - N_GROUPS
        w_r = jnp.concatenate([w_re[l], w_rg[l], jnp.zeros((d, pad), F32)], axis=1).astype(BF16)
        b_r = jnp.concatenate([b_re[l], b_rg[l], jnp.zeros((pad,), F32)]).reshape(1, LANES)
        p = dict(norm_mix=norm_mix[l], w_in=w_in[l].astype(BF16), sg_norm=sg_norm[l], sg_w=sg_w[l], sg_b=sg_b[l],
                 conv_w=conv_w[l], qn_c=qn_c[l], kn_c=kn_c[l], qn_d=qn_d[l], kn_d=kn_d[l], lam_q1=lam_q1[l],
                 lam_k1=lam_k1[l], lam_q2=lam_q2[l], lam_k2=lam_k2[l], subln_d=subln_d[l],
                 w_gate=w_gate[l].astype(BF16), b_gate=b_gate[l], w_branch=w_branch[l].astype(BF16),
                 w_out=w_out[l].astype(BF16), norm_ffn=norm_ffn[l], w_r=w_r, b_r=b_r,
                 w_up=w_up, w_down=w_down, layer=l)
        xf = layer(xf, p, 0.8 - 0.6 * math.exp(-0.3 * l), batch, tabs)
    return xf.reshape(batch, seq, d)
```

```python
import functools
import math

import jax
import jax.numpy as jnp
from jax import lax
from jax.experimental import pallas as pl
from jax.experimental.pallas import tpu as pltpu

F32 = jnp.float32
BF16 = jnp.bfloat16
I32 = jnp.int32
U32 = jnp.uint32

D_MODEL = 2048
BRANCH_W = 512
HEAD_DIM = 128
N_HEADS = 4
SG_CHUNK = 128
DIL_CFG = ((128, 1), (512, 4), (2048, 16))
N_BACK = 128
DIFF_DIM = 64
ROPE_THETA = 10000.0
W_A = 2 * BRANCH_W
W_B = 3 * BRANCH_W
W_C = 9 * BRANCH_W
N_GROUPS = 4
EXP_PER_GROUP = 8
N_EXPERTS = 32
D_EXPERT = 512
EPS = 1e-6
NEG = -1e30

LANES = 128
PACK_ROWS = D_MODEL // 2 // LANES
VMEM_LIMIT = 56 * 1024 * 1024
ROW_TILE = 512
DIFF_TILE = 256
DIFF_KTILE = 2 * DIFF_TILE
ROUTE_TILE = 512
ROUTE_SPLIT = 2
DMA_UNROLL = 8


def _cparams(sem, **kw):
    return pltpu.CompilerParams(dimension_semantics=sem, vmem_limit_bytes=VMEM_LIMIT, **kw)


def _sds(shape, dtype):
    return jax.ShapeDtypeStruct(shape, dtype)


def _proj_body(x_ref, g_ref, w_ref, z_ref, h_ref):
    @pl.when(pl.program_id(1) == 0)
    def _():
        x = x_ref[...]
        ms = jnp.mean(x * x, axis=-1, keepdims=True)
        h_ref[...] = (x * lax.rsqrt(ms + EPS) * g_ref[...]).astype(BF16)

    z_ref[...] = jnp.dot(h_ref[...], w_ref[...], preferred_element_type=F32).astype(z_ref.dtype)


def proj(x, g, w, tm=1024, tn=512):
    t, d = x.shape
    n = w.shape[1]
    return pl.pallas_call(
        _proj_body,
        grid=(t // tm, n // tn),
        in_specs=[pl.BlockSpec((tm, d), lambda i, j: (i, 0)),
                  pl.BlockSpec((1, d), lambda i, j: (0, 0)),
                  pl.BlockSpec((d, tn), lambda i, j: (0, j))],
        out_specs=[pl.BlockSpec((tm, tn), lambda i, j: (i, j)),
                   pl.BlockSpec((tm, d), lambda i, j: (i, 0))],
        out_shape=[_sds((t, n), BF16), _sds((t, d), BF16)],
        compiler_params=_cparams(("arbitrary", "arbitrary")),
        name="proj",
    )(x, g.reshape(1, d), w)


def _gelu_tanh(x):
    c = math.sqrt(2.0 / math.pi)
    return 0.5 * x * (1.0 + jnp.tanh(c * (x + 0.044715 * (x * x * x))))


def _mixa_body(z_ref, g_ref, w_ref, b_ref, o_ref):
    tm = z_ref.shape[0]
    ga = _gelu_tanh(z_ref[...].astype(F32))
    u = ga[:, :BRANCH_W]
    v = ga[:, BRANCH_W:]
    ms = jnp.mean(v * v, axis=-1, keepdims=True)
    vn = (v * lax.rsqrt(ms + EPS) * g_ref[...]).astype(BF16)
    row = lax.broadcasted_iota(I32, (SG_CHUNK, SG_CHUNK), 0)
    col = lax.broadcasted_iota(I32, (SG_CHUNK, SG_CHUNK), 1)
    for g in range(N_HEADS):
        wg = jnp.where(row >= col, w_ref[g], 0.0).astype(BF16)
        bias = b_ref[g]
        cs = slice(g * SG_CHUNK, (g + 1) * SG_CHUNK)
        for c in range(tm // SG_CHUNK):
            rs = slice(c * SG_CHUNK, (c + 1) * SG_CHUNK)
            mixed = jnp.dot(wg, vn[rs, cs], preferred_element_type=F32) + bias
            o_ref[rs, cs] = (u[rs, cs] * mixed).astype(o_ref.dtype)


def mix_a(z, sg_norm, sg_w, sg_b, tm=512):
    t = z.shape[0]
    bias = jnp.broadcast_to(sg_b[:, :, None], (N_HEADS, SG_CHUNK, SG_CHUNK)).astype(F32)
    return pl.pallas_call(
        _mixa_body,
        grid=(t // tm,),
        in_specs=[pl.BlockSpec((tm, W_A), lambda i: (i, 0)),
                  pl.BlockSpec((1, BRANCH_W), lambda i: (0, 0)),
                  pl.BlockSpec((N_HEADS, SG_CHUNK, SG_CHUNK), lambda i: (0, 0, 0)),
                  pl.BlockSpec((N_HEADS, SG_CHUNK, SG_CHUNK), lambda i: (0, 0, 0))],
        out_specs=pl.BlockSpec((tm, BRANCH_W), lambda i: (i, 0)),
        out_shape=_sds((t, BRANCH_W), BF16),
        compiler_params=_cparams(("arbitrary",)),
        name="mix_a",
    )(z, sg_norm.reshape(1, BRANCH_W), sg_w, bias)


def _mixb_body(bg_ref, cg_ref, hb_ref, w_ref, o_ref, prev_ref):
    @pl.when(pl.program_id(1) == 0)
    def _():
        prev_ref[...] = jnp.zeros_like(prev_ref)

    zz = cg_ref[...].astype(F32) * hb_ref[...].astype(F32)
    tm = zz.shape[0]
    row = lax.broadcasted_iota(I32, zz.shape, 0)
    p = prev_ref[...]
    z1 = jnp.where(row == 0, p[7:8, :], pltpu.roll(zz, 1, 0))
    z2 = jnp.where(row == 0, p[6:7, :], jnp.where(row == 1, p[7:8, :], pltpu.roll(zz, 2, 0)))
    w = w_ref[...]
    y = w[0:1, :] * z2 + w[1:2, :] * z1 + w[2:3, :] * zz
    o_ref[...] = (bg_ref[...].astype(F32) * y).astype(o_ref.dtype)
    prev_ref[...] = zz[tm - 8:, :]


def mix_b(z, conv_w, batch, tm=512):
    t = z.shape[0]
    nt = t // batch // tm
    c0 = W_A // BRANCH_W

    def zspec(k):
        return pl.BlockSpec((tm, BRANCH_W), lambda b, i: (b * nt + i, c0 + k))

    return pl.pallas_call(
        _mixb_body,
        grid=(batch, nt),
        in_specs=[zspec(0), zspec(1), zspec(2), pl.BlockSpec((3, BRANCH_W), lambda b, i: (0, 0))],
        out_specs=pl.BlockSpec((tm, BRANCH_W), lambda b, i: (b * nt + i, 0)),
        out_shape=_sds((t, BRANCH_W), BF16),
        scratch_shapes=[pltpu.VMEM((8, BRANCH_W), F32)],
        compiler_params=_cparams(("arbitrary", "arbitrary")),
        name="mix_b",
    )(z, z, z, conv_w)


def rope_tables(s, half):
    inv = ROPE_THETA ** (-jnp.arange(half, dtype=F32) / half)
    ang = jnp.arange(s, dtype=F32)[:, None] * inv[None, :]
    cos, sin = jnp.cos(ang), jnp.sin(ang)
    reps = LANES // (2 * half)
    cos_t = jnp.tile(jnp.concatenate([cos, cos], axis=-1), (1, reps))
    sin_t = jnp.tile(jnp.concatenate([-sin, sin], axis=-1), (1, reps))
    return cos_t, sin_t


def _group_matrices(width):
    r = lax.broadcasted_iota(I32, (LANES, LANES), 0)
    c = lax.broadcasted_iota(I32, (LANES, LANES), 1)
    partner = (c // width) * width + (c % width + width // 2) % width
    ones = jnp.where((r // width) == (c // width), 1.0, 0.0).astype(BF16)
    perm = jnp.where(r == partner, 1.0, 0.0).astype(BF16)
    return ones, perm


def _norm_rope(xh, gain, cos, sin, ones, perm, width):
    ssq = jnp.dot((xh * xh).astype(BF16), ones, preferred_element_type=F32)
    xn = xh * lax.rsqrt(ssq * (1.0 / width) + EPS) * gain
    rot = jnp.dot(xn.astype(BF16), perm, preferred_element_type=F32)
    return xn * cos + rot * sin


def _prep_c_body(dil, zq_ref, zk_ref, zv_ref, gq_ref, gk_ref, cos_ref, sin_ref, o_ref, scr_ref):
    tm = zq_ref.shape[0]
    cos = cos_ref[...]
    sin = sin_ref[...]

    ones, perm = _group_matrices(HEAD_DIM)

    def norm_rope(xh, gain):
        return _norm_rope(xh, gain, cos, sin, ones, perm, HEAD_DIM)

    width = 3 * BRANCH_W
    for w, ref in enumerate((zq_ref, zk_ref, zv_ref)):
        for h in range(N_HEADS):
            xh = ref[:, h * HEAD_DIM:(h + 1) * HEAD_DIM].astype(F32)
            val = xh if w == 2 else norm_rope(xh, (gq_ref, gk_ref)[w][...])
            c = w * BRANCH_W + h * HEAD_DIM
            if dil == 1:
                o_ref[0, :, c:c + HEAD_DIM] = val.astype(o_ref.dtype)
            else:
                buf = scr_ref.at[w * N_HEADS + h]
                buf[...] = val
                for r in range(dil):
                    o_ref[0, :, r * width + c:r * width + c + HEAD_DIM] = (
                        buf[pl.ds(r, tm // dil, stride=dil), :].astype(o_ref.dtype))


def prep_c(z, g, dil, qn, kn, cos_t, sin_t, batch, tm=512):
    t = z.shape[0]
    seq = t // batch
    ns = seq // tm
    c0 = (W_A + W_B) // BRANCH_W + g
    gq = (qn * (HEAD_DIM ** -0.5 * math.log2(math.e))).reshape(1, HEAD_DIM).astype(F32)
    gk = kn.reshape(1, HEAD_DIM).astype(F32)
    vec = pl.BlockSpec((1, HEAD_DIM), lambda i: (0, 0))
    tab = pl.BlockSpec((tm, LANES), lambda i: (i % ns, 0))
    return pl.pallas_call(
        functools.partial(_prep_c_body, dil),
        grid=(t // tm,),
        in_specs=[pl.BlockSpec((tm, BRANCH_W), lambda i: (i, c0)),
                  pl.BlockSpec((tm, BRANCH_W), lambda i: (i, c0 + 3)),
                  pl.BlockSpec((tm, BRANCH_W), lambda i: (i, c0 + 6)), vec, vec, tab, tab],
        out_specs=pl.BlockSpec((1, tm // dil, dil * 3 * BRANCH_W), lambda i: (i // ns, i % ns, 0)),
        out_shape=_sds((batch, seq // dil, dil * 3 * BRANCH_W), BF16),
        scratch_shapes=[pltpu.VMEM((3 * N_HEADS, tm, HEAD_DIM), F32)],
        compiler_params=_cparams(("arbitrary",)),
        name="prep_c",
    )(z, z, z, gq, gk, cos_t, sin_t)


def _attc_body(q_ref, kc_ref, kp_ref, vc_ref, vp_ref, o_ref, lse_ref):
    i = pl.program_id(1)
    nsub = q_ref.shape[1] // N_BACK
    qi = lax.broadcasted_iota(I32, (N_BACK, N_BACK), 0)
    kj = lax.broadcasted_iota(I32, (N_BACK, N_BACK), 1)
    ok_own = kj <= qi
    ok_before = kj >= qi
    ok_first = (kj + jnp.where(i > 0, 0, -2 * N_BACK)) >= qi
    dn = (((1,), (1,)), ((), ()))
    heads = [slice(h * HEAD_DIM, (h + 1) * HEAD_DIM) for h in range(N_HEADS)]
    subs = [slice(u * N_BACK, (u + 1) * N_BACK) for u in range(nsub)]
    scores = []
    for h in heads:
        for u in range(nsub):
            q = q_ref[0, subs[u], h]
            k_before = kp_ref[0, :, h] if u == 0 else kc_ref[0, subs[u - 1], h]
            s_before = lax.dot_general(q, k_before, dn, preferred_element_type=F32)
            s_own = lax.dot_general(q, kc_ref[0, subs[u], h], dn, preferred_element_type=F32)
            scores.append((jnp.where(ok_first if u == 0 else ok_before, s_before, NEG),
                           jnp.where(ok_own, s_own, NEG)))
    probs = []
    for s_before, s_own in scores:
        m = jnp.max(jnp.maximum(s_before, s_own), axis=-1, keepdims=True)
        probs.append((jnp.exp2(s_before - m).astype(BF16), jnp.exp2(s_own - m).astype(BF16), m))
    ones = jnp.ones((N_BACK, HEAD_DIM), BF16)
    n = 0
    for h in heads:
        for u in range(nsub):
            p_before, p_own, m = probs[n]
            n += 1
            v_before = vp_ref[0, :, h] if u == 0 else vc_ref[0, subs[u - 1], h]
            ol = (jnp.dot(p_before, jnp.concatenate([v_before, ones], axis=1), preferred_element_type=F32)
                  + jnp.dot(p_own, jnp.concatenate([vc_ref[0, subs[u], h], ones], axis=1),
                            preferred_element_type=F32))
            l = ol[:, HEAD_DIM:]
            o_ref[0, subs[u], h] = (ol[:, :HEAD_DIM] / l).astype(o_ref.dtype)
            lse_ref[0, subs[u], h] = m + jnp.log2(l)


def att_c(qkv, dil, tq=256):
    b, length, _ = qkv.shape
    r = tq // N_BACK

    def cur(w):
        return pl.BlockSpec((1, tq, BRANCH_W), lambda n, i: (n // dil, i, (n % dil) * 3 + w))

    def prev(w):
        return pl.BlockSpec((1, N_BACK, BRANCH_W),
                            lambda n, i: (n // dil, jnp.maximum(i * r - 1, 0), (n % dil) * 3 + w))

    out = pl.BlockSpec((1, tq, BRANCH_W), lambda n, i: (n // dil, i, n % dil))
    return pl.pallas_call(
        _attc_body,
        grid=(b * dil, length // tq),
        in_specs=[cur(0), cur(1), prev(1), cur(2), prev(2)],
        out_specs=[out, out],
        out_shape=[_sds((b, length, dil * BRANCH_W), BF16), _sds((b, length, dil * BRANCH_W), F32)],
        compiler_params=_cparams(("arbitrary", "arbitrary")),
        name="att_c",
    )(qkv, qkv, qkv, qkv, qkv)


def _merge_c_body(dils, *refs):
    n = len(dils)
    o_refs, l_refs, y_ref, scr = refs[:n], refs[n:2 * n], refs[2 * n], refs[2 * n + 1:]
    tm = y_ref.shape[0]

    def natural(ref, dil, buf):
        if dil == 1:
            return ref[0].astype(F32)
        for h in range(N_HEADS):
            for r in range(dil):
                c = r * BRANCH_W + h * HEAD_DIM
                buf[h, pl.ds(r, tm // dil, stride=dil), :] = ref[0, :, c:c + HEAD_DIM].astype(F32)
        return jnp.concatenate([buf[h] for h in range(N_HEADS)], axis=-1)

    outs = [natural(o_refs[g], dils[g], scr[2 * g]) for g in range(n)]
    lses = [natural(l_refs[g], dils[g], scr[2 * g + 1]) for g in range(n)]
    m = functools.reduce(jnp.maximum, lses)
    es = [jnp.exp2(l - m) for l in lses]
    y = functools.reduce(lambda a, b: a + b, [e * o for e, o in zip(es, outs)])
    y_ref[...] = (y / functools.reduce(lambda a, b: a + b, es)).astype(y_ref.dtype)


def merge_c(outs, lses, dils, tm=512):
    batch, seq = outs[0].shape[0], outs[0].shape[1] * dils[0]
    t = batch * seq
    ns = seq // tm
    specs = [pl.BlockSpec((1, tm // d, d * BRANCH_W), lambda i: (i // ns, i % ns, 0)) for d in dils]
    return pl.pallas_call(
        functools.partial(_merge_c_body, tuple(dils)),
        grid=(t // tm,),
        in_specs=specs + specs,
        out_specs=pl.BlockSpec((tm, BRANCH_W), lambda i: (i, 0)),
        out_shape=_sds((t, BRANCH_W), BF16),
        scratch_shapes=[pltpu.VMEM((N_HEADS, tm, HEAD_DIM), F32) for _ in range(2 * len(dils))],
        compiler_params=_cparams(("arbitrary",)),
        name="merge_c",
    )(*outs, *lses)


def _prep_d_body(zq_ref, zk_ref, zv_ref, gq_ref, gk_ref, cos_ref, sin_ref, q1_ref, q2_ref, k_ref, vt_ref):
    tm = zq_ref.shape[0]
    lane = lax.broadcasted_iota(I32, (tm, LANES), 1)
    lo = lane < DIFF_DIM
    cos = cos_ref[...]
    sin = sin_ref[...]
    zq = zq_ref[...].astype(F32)
    zk = zk_ref[...].astype(F32)
    ones, perm = _group_matrices(DIFF_DIM)
    for h in range(N_HEADS):
        sl = slice(h * HEAD_DIM, (h + 1) * HEAD_DIM)
        q = _norm_rope(zq[:, sl], gq_ref[...], cos, sin, ones, perm, DIFF_DIM)
        q1_ref[:, sl] = jnp.where(lo, q, 0.0).astype(BF16)
        q2_ref[:, sl] = jnp.where(lo, 0.0, q).astype(BF16)
        k_ref[:, sl] = _norm_rope(zk[:, sl], gk_ref[...], cos, sin, ones, perm, DIFF_DIM).astype(BF16)
    vt_ref[0] = zv_ref[...].astype(F32).T.astype(BF16)


def prep_d(z, qn, kn, cos_t, sin_t, seq, tm=DIFF_KTILE):
    t = z.shape[0]
    ns = seq // tm
    c0 = (W_A + W_B + W_C) // BRANCH_W
    gq = jnp.tile(qn * (DIFF_DIM ** -0.5 * math.log2(math.e)), 2).reshape(1, LANES).astype(F32)
    gk = jnp.tile(kn, 2).reshape(1, LANES).astype(F32)
    row = pl.BlockSpec((tm, BRANCH_W), lambda i: (i, 0))
    tab = pl.BlockSpec((tm, LANES), lambda i: (i % ns, 0))
    vec = pl.BlockSpec((1, LANES), lambda i: (0, 0))
    return pl.pallas_call(
        _prep_d_body,
        grid=(t // tm,),
        in_specs=[pl.BlockSpec((tm, BRANCH_W), lambda i: (i, c0)),
                  pl.BlockSpec((tm, BRANCH_W), lambda i: (i, c0 + 1)),
                  pl.BlockSpec((tm, BRANCH_W), lambda i: (i, c0 + 2)), vec, vec, tab, tab],
        out_specs=[row, row, row, pl.BlockSpec((1, BRANCH_W, tm), lambda i: (i, 0, 0))],
        out_shape=[_sds((t, BRANCH_W), BF16)] * 3 + [_sds((t // tm, BRANCH_W, tm), BF16)],
        compiler_params=_cparams(("arbitrary",)),
        name="prep_d",
    )(z, z, z, gq, gk, cos_t, sin_t)


def _diff_body(lam_init, q1_ref, q2_ref, k_ref, vt_ref, lq1, lk1, lq2, lk2, sg_ref, o_ref, acc_ref):
    i = pl.program_id(1)
    tq = q1_ref.shape[1]
    n_chain = 2 * N_HEADS
    lam = (jnp.exp(jnp.sum(lq1[...] * lk1[...], axis=-1, keepdims=True))
           - jnp.exp(jnp.sum(lq2[...] * lk2[...], axis=-1, keepdims=True)) + lam_init)
    tk = vt_ref.shape[2]
    dn = (((1,), (1,)), ((), ()))
    acc_ref[...] = jnp.zeros_like(acc_ref)
    heads = [slice(h * HEAD_DIM, (h + 1) * HEAD_DIM) for h in range(N_HEADS)]

    def step(j, carry, n_keys, shift):
        ms, ls = carry
        start = pl.multiple_of(j * tk, tk)
        scores = []
        for c in range(n_chain):
            q = (q1_ref, q2_ref)[c % 2][0, :, heads[c // 2]]
            scores.append(lax.dot_general(k_ref[0, pl.ds(start, n_keys), heads[c // 2]], q, dn,
                                          preferred_element_type=F32))
        if shift is not None:
            krow = lax.broadcasted_iota(I32, (n_keys, tq), 0)
            qcol = lax.broadcasted_iota(I32, (n_keys, tq), 1)
            visible = krow <= qcol + shift
        new_m, new_l, alphas, probs = [], [], [], []
        for c in range(n_chain):
            s = scores[c] if shift is None else jnp.where(visible, scores[c], NEG)
            m_new = jnp.maximum(ms[c], jnp.max(s, axis=0, keepdims=True))
            a = jnp.exp2(ms[c] - m_new)
            p = jnp.exp2(s - m_new)
            new_l.append(a * ls[c] + jnp.sum(p, axis=0, keepdims=True))
            new_m.append(m_new)
            alphas.append(a)
            probs.append(p.astype(BF16))
        for c in range(n_chain):
            pv = jnp.dot(vt_ref[j, heads[c // 2], 0:n_keys], probs[c], preferred_element_type=F32)
            acc_ref[c] = alphas[c] * acc_ref[c] + pv
        return tuple(new_m), tuple(new_l)

    init = (tuple(jnp.full((1, tq), NEG, F32) for _ in range(n_chain)),
            tuple(jnp.zeros((1, tq), F32) for _ in range(n_chain)))
    n_full = i // 2
    carry = lax.fori_loop(0, n_full, functools.partial(step, n_keys=tk, shift=None), init)
    _, ls = lax.cond(i % 2 == 1,
                     lambda c: step(n_full, c, tk, tq),
                     lambda c: step(n_full, c, tq, 0), carry)
    for h in range(N_HEADS):
        o1 = acc_ref[2 * h] * (1.0 / ls[2 * h])
        o2 = acc_ref[2 * h + 1] * (1.0 / ls[2 * h + 1])
        a = o1 - lam * o2
        ms = jnp.mean(a * a, axis=0, keepdims=True)
        y = a * lax.rsqrt(ms + EPS) * sg_ref[...]
        o_ref[0, :, h * HEAD_DIM:(h + 1) * HEAD_DIM] = y.T.astype(o_ref.dtype)


def diff_att(q1, q2, k, vt, lq1, lk1, lq2, lk2, sub_g, lam_init, tq=DIFF_TILE):
    b, s, _ = q1.shape
    tk = vt.shape[2]
    assert tk == 2 * tq
    nk = s // tq
    qs = pl.BlockSpec((1, tq, BRANCH_W), lambda n, i: (n, i, 0))
    vec = pl.BlockSpec((1, DIFF_DIM), lambda n, i: (0, 0))
    sg = jnp.broadcast_to((sub_g * (1.0 - lam_init))[:, None], (HEAD_DIM, tq)).astype(F32)
    return pl.pallas_call(
        functools.partial(_diff_body, lam_init),
        grid=(b, nk),
        in_specs=[qs, qs,
                  pl.BlockSpec((1, s, BRANCH_W), lambda n, i: (n, 0, 0)),
                  pl.BlockSpec((s // tk, BRANCH_W, tk), lambda n, i: (n, 0, 0)),
                  vec, vec, vec, vec,
                  pl.BlockSpec((HEAD_DIM, tq), lambda n, i: (0, 0))],
        out_specs=qs,
        out_shape=_sds((b, s, BRANCH_W), BF16),
        scratch_shapes=[pltpu.VMEM((2 * N_HEADS, HEAD_DIM, tq), F32)],
        compiler_params=_cparams(("arbitrary", "arbitrary")),
        name="diff_att",
    )(q1, q2, k, vt, lq1.reshape(1, -1), lk1.reshape(1, -1), lq2.reshape(1, -1), lk2.reshape(1, -1), sg)


def _gate_body(h_ref, ya, yb, yc, yd, wg_ref, bg_ref, wb_ref, o_ref):
    h = h_ref[...]
    acc = None
    for i, y in enumerate((ya, yb, yc, yd)):
        g = jax.nn.sigmoid(jnp.dot(h, wg_ref[i], preferred_element_type=F32) + bg_ref[i:i + 1, :])
        t = jnp.dot(y[...], wb_ref[i], preferred_element_type=F32)
        acc = g * t if acc is None else acc + g * t
    o_ref[...] = acc.astype(o_ref.dtype)


def gate_merge(h, ys, w_gate, b_gate, w_branch, tm=512, tn=512):
    t, d = h.shape
    ysp = pl.BlockSpec((tm, BRANCH_W), lambda j, i: (i, 0))
    return pl.pallas_call(
        _gate_body,
        grid=(d // tn, t // tm),
        in_specs=[pl.BlockSpec((tm, d), lambda j, i: (i, 0)), ysp, ysp, ysp, ysp,
                  pl.BlockSpec((4, d, tn), lambda j, i: (0, 0, j)),
                  pl.BlockSpec((4, tn), lambda j, i: (0, j)),
                  pl.BlockSpec((4, BRANCH_W, tn), lambda j, i: (0, 0, j))],
        out_specs=pl.BlockSpec((tm, tn), lambda j, i: (i, j)),
        out_shape=_sds((t, d), BF16),
        compiler_params=_cparams(("arbitrary", "arbitrary")),
        name="gate_merge",
    )(h, *ys, w_gate, b_gate, w_branch)


def _pack_rows(x):
    half = x.shape[1] // 2
    lo = pltpu.bitcast(x[:, :half].astype(BF16).astype(F32), U32)
    hi = pltpu.bitcast(x[:, half:].astype(BF16).astype(F32), U32)
    return (lo >> 16) | hi


def _store_packed(ref, words):
    m = words.shape[0]
    for c in range(PACK_ROWS):
        ref[pl.ds(c, m, stride=PACK_ROWS), :] = words[:, c * LANES:(c + 1) * LANES]


def _load_packed(ref, m):
    words = jnp.concatenate([ref[pl.ds(c, m, stride=PACK_ROWS), :] for c in range(PACK_ROWS)], axis=-1)
    lo = pltpu.bitcast(words << 16, F32)
    hi = pltpu.bitcast(words & jnp.uint32(0xFFFF0000), F32)
    return lo, hi


def _outproj_body(m_ref, x_ref, wo_ref, g_ref, wr_ref, br_ref, x1_ref, hp_ref, meta_ref, cnt_ref, imeta_ref,
                  run_ref):
    @pl.when(pl.program_id(0) == 0)
    def _():
        run_ref[...] = jnp.zeros_like(run_ref)

    sub = x_ref.shape[0] // ROUTE_SPLIT
    x1s = []
    for s in range(ROUTE_SPLIT):
        rows = slice(s * sub, (s + 1) * sub)
        x1 = x_ref[rows, :] + jnp.dot(m_ref[rows, :], wo_ref[...], preferred_element_type=F32)
        x1_ref[rows, :] = x1
        x1s.append(x1)
    run = run_ref[0:1, :]
    for s in range(ROUTE_SPLIT):
        rows = slice(s * sub, (s + 1) * sub)
        ms = jnp.mean(x1s[s] * x1s[s], axis=-1, keepdims=True)
        h2 = x1s[s] * lax.rsqrt(ms + EPS) * g_ref[...]
        _store_packed(hp_ref.at[pl.ds(s * sub * PACK_ROWS, sub * PACK_ROWS), :], _pack_rows(h2))
        meta, run = _route(h2, wr_ref, br_ref, run)
        meta_ref[rows, :] = meta
        imeta_ref[0, :, rows] = meta.T[0:8, :].astype(I32)
    run_ref[...] = jnp.broadcast_to(run, run_ref.shape)
    cnt_ref[...] = jnp.broadcast_to(run, cnt_ref.shape)


def _route(h2, wr_ref, br_ref, run):
    tm = h2.shape[0]
    lg = jnp.dot(h2.astype(BF16), wr_ref[...], preferred_element_type=F32) + br_ref[...]
    lane = lax.broadcasted_iota(I32, (tm, LANES), 1)
    big = jnp.int32(4 * LANES)
    is_g = (lane >= N_EXPERTS) & (lane < N_EXPERTS + N_GROUPS)
    gl = jnp.where(is_g, lg, NEG)
    gm = jnp.max(gl, axis=-1, keepdims=True)
    g_p = 1.0 / jnp.sum(jnp.exp(gl - gm), axis=-1, keepdims=True)
    gidx = jnp.min(jnp.where(gl == gm, lane, big), axis=-1, keepdims=True) - N_EXPERTS
    lo_e = gidx * EXP_PER_GROUP
    in_g = (lane >= lo_e) & (lane < lo_e + EXP_PER_GROUP)
    el = jnp.where(in_g, lg, NEG)
    v1 = jnp.max(el, axis=-1, keepdims=True)
    i1 = jnp.min(jnp.where(el == v1, lane, big), axis=-1, keepdims=True)
    el2 = jnp.where(lane == i1, NEG, el)
    v2 = jnp.max(el2, axis=-1, keepdims=True)
    i2 = jnp.min(jnp.where(el2 == v2, lane, big), axis=-1, keepdims=True)
    e2 = jnp.exp(v2 - v1)
    w1 = g_p / (1.0 + e2)
    w2 = g_p * e2 / (1.0 + e2)

    oh1 = lane == i1
    oh2 = lane == i2
    a = jnp.where(oh1, 1.0, jnp.where(oh2, 1.0, 0.0))
    r = lax.broadcasted_iota(I32, (tm, tm), 0)
    c = lax.broadcasted_iota(I32, (tm, tm), 1)
    before = jnp.where(c < r, 1.0, 0.0).astype(BF16)
    pref = jnp.dot(before, a.astype(BF16), preferred_element_type=F32) + run
    rank1 = jnp.sum(jnp.where(oh1, pref, 0.0), axis=-1, keepdims=True)
    rank2 = jnp.sum(jnp.where(oh2, pref, 0.0), axis=-1, keepdims=True)

    meta = jnp.where(lane == 0, i1.astype(F32), 0.0)
    meta = jnp.where(lane == 1, i2.astype(F32), meta)
    meta = jnp.where(lane == 2, w1, meta)
    meta = jnp.where(lane == 3, w2, meta)
    meta = jnp.where(lane == 4, rank1, meta)
    meta = jnp.where(lane == 5, rank2, meta)
    return meta, run + jnp.sum(a, axis=0, keepdims=True)


def outproj_route(merged, x, w_out, g2, w_r, b_r, tm=ROUTE_TILE):
    t, d = x.shape
    return pl.pallas_call(
        _outproj_body,
        grid=(t // tm,),
        in_specs=[pl.BlockSpec((tm, d), lambda i: (i, 0)),
                  pl.BlockSpec((tm, d), lambda i: (i, 0)),
                  pl.BlockSpec((d, d), lambda i: (0, 0)),
                  pl.BlockSpec((1, d), lambda i: (0, 0)),
                  pl.BlockSpec((d, LANES), lambda i: (0, 0)),
                  pl.BlockSpec((1, LANES), lambda i: (0, 0))],
        out_specs=[pl.BlockSpec((tm, d), lambda i: (i, 0)),
                   pl.BlockSpec((tm * PACK_ROWS, LANES), lambda i: (i, 0)),
                   pl.BlockSpec((tm, LANES), lambda i: (i, 0)),
                   pl.BlockSpec((8, LANES), lambda i: (0, 0)),
                   pl.BlockSpec((1, 8, tm), lambda i: (i, 0, 0))],
        out_shape=[_sds((t, d), F32), _sds((t * PACK_ROWS, LANES), U32), _sds((t, LANES), F32),
                   _sds((8, LANES), F32), _sds((t // tm, 8, tm), I32)],
        scratch_shapes=[pltpu.VMEM((8, LANES), F32)],
        compiler_params=_cparams(("arbitrary",)),
        name="outproj_route",
    )(merged, x, w_out, g2.reshape(1, d), w_r, b_r)


def _slot(seg_ref, imeta_ref, k, t):
    return seg_ref[imeta_ref[0, k, t]] + imeta_ref[0, 4 + k, t]


def _dispatch_body(seg_ref, imeta_ref, hp_ref, xs_ref, sem):
    tm = hp_ref.shape[0] // PACK_ROWS

    def row_copy(t, k):
        pos = _slot(seg_ref, imeta_ref, k, t)
        src = hp_ref.at[pl.ds(pl.multiple_of(t * PACK_ROWS, PACK_ROWS), PACK_ROWS), :]
        dst = xs_ref.at[pl.ds(pl.multiple_of(pos * PACK_ROWS, PACK_ROWS), PACK_ROWS), :]
        return pltpu.make_async_copy(src, dst, sem)

    def start(t, carry):
        row_copy(t, 0).start()
        row_copy(t, 1).start()
        return carry

    lax.fori_loop(0, tm, start, 0, unroll=DMA_UNROLL)
    whole = pltpu.make_async_copy(hp_ref, xs_ref.at[pl.ds(0, tm * PACK_ROWS), :], sem)
    whole.wait()
    whole.wait()


def dispatch(hp, seg_start, imeta, rows):
    nt, _, tm = imeta.shape
    grid_spec = pltpu.PrefetchScalarGridSpec(
        num_scalar_prefetch=1,
        grid=(nt,),
        in_specs=[pl.BlockSpec((1, 8, tm), lambda i, seg: (i, 0, 0), memory_space=pltpu.SMEM),
                  pl.BlockSpec((tm * PACK_ROWS, LANES), lambda i, seg: (i, 0))],
        out_specs=pl.BlockSpec(memory_space=pl.ANY),
        scratch_shapes=[pltpu.SemaphoreType.DMA(())],
    )
    return pl.pallas_call(
        _dispatch_body,
        grid_spec=grid_spec,
        out_shape=_sds((rows * PACK_ROWS, LANES), U32),
        compiler_params=_cparams(("arbitrary",), has_side_effects=True),
        name="dispatch",
    )(seg_start, imeta, hp)


def _expert_body(te_ref, tv_ref, xs_ref, wu_ref, wd_ref, ys_ref, wu_bf, wd_bf):
    i = pl.program_id(0)
    tr = xs_ref.shape[0] // PACK_ROWS

    @pl.when(jnp.logical_or(i == 0, te_ref[i] != te_ref[jnp.maximum(i - 1, 0)]))
    def _():
        wu_bf[...] = wu_ref[0, 0].astype(BF16)
        wd_bf[...] = wd_ref[0, 0].astype(BF16)

    @pl.when(tv_ref[i] > 0)
    def _():
        lo, hi = _load_packed(xs_ref, tr)
        x = jnp.concatenate([lo, hi], axis=-1).astype(BF16)
        ac = jnp.dot(x, wu_bf[...], preferred_element_type=F32)
        a = ac[:, :D_EXPERT]
        c = ac[:, D_EXPERT:]
        hmid = (a * jax.nn.sigmoid(a) * c).astype(BF16)
        y = jnp.dot(hmid, wd_bf[...], preferred_element_type=F32)
        _store_packed(ys_ref, _pack_rows(y))


def experts(xs, tile_expert, tile_valid, w_up, w_down, layer_idx, tr=ROW_TILE):
    rows = xs.shape[0] // PACK_ROWS
    nt = rows // tr
    grid_spec = pltpu.PrefetchScalarGridSpec(
        num_scalar_prefetch=2,
        grid=(nt,),
        in_specs=[pl.BlockSpec((tr * PACK_ROWS, LANES), lambda i, te, tv: (i, 0)),
                  pl.BlockSpec((1, 1, D_MODEL, 2 * D_EXPERT), lambda i, te, tv: (layer_idx, te[i], 0, 0)),
                  pl.BlockSpec((1, 1, D_EXPERT, D_MODEL), lambda i, te, tv: (layer_idx, te[i], 0, 0))],
        out_specs=pl.BlockSpec((tr * PACK_ROWS, LANES), lambda i, te, tv: (i, 0)),
        scratch_shapes=[pltpu.VMEM((D_MODEL, 2 * D_EXPERT), BF16), pltpu.VMEM((D_EXPERT, D_MODEL), BF16)],
    )
    return pl.pallas_call(
        _expert_body,
        grid_spec=grid_spec,
        out_shape=_sds((rows * PACK_ROWS, LANES), U32),
        compiler_params=_cparams(("arbitrary",)),
        name="experts",
    )(tile_expert, tile_valid, xs, w_up, w_down)


def _combine_body(seg_ref, imeta_ref, imeta_next_ref, ys_ref, x_ref, meta_ref, o_ref, buf, sem):
    i = pl.program_id(0)
    tm = x_ref.shape[0]
    slot = i % 2

    def gather(im_ref, dst_slot):
        def start(t, carry):
            for k in range(2):
                pos = _slot(seg_ref, im_ref, k, t)
                src = ys_ref.at[pl.ds(pl.multiple_of(pos * PACK_ROWS, PACK_ROWS), PACK_ROWS), :]
                dst = buf.at[dst_slot, k, pl.ds(pl.multiple_of(t * PACK_ROWS, PACK_ROWS), PACK_ROWS), :]
                pltpu.make_async_copy(src, dst, sem.at[dst_slot]).start()
            return carry

        lax.fori_loop(0, tm, start, 0, unroll=DMA_UNROLL)

    @pl.when(i == 0)
    def _():
        gather(imeta_ref, 0)

    @pl.when(i + 1 < pl.num_programs(0))
    def _():
        gather(imeta_next_ref, 1 - slot)

    for k in range(2):
        pltpu.make_async_copy(ys_ref.at[pl.ds(0, tm * PACK_ROWS), :], buf.at[slot, k], sem.at[slot]).wait()
    meta = meta_ref[...]
    w1 = meta[:, 2:3]
    w2 = meta[:, 3:4]
    half = D_MODEL // 2
    lo1, hi1 = _load_packed(buf.at[slot, 0], tm)
    lo2, hi2 = _load_packed(buf.at[slot, 1], tm)
    o_ref[:, :half] = x_ref[:, :half] + w1 * lo1 + w2 * lo2
    o_ref[:, half:] = x_ref[:, half:] + w1 * hi1 + w2 * hi2


def combine(ys, seg_start, imeta, x1, meta):
    t, d = x1.shape
    nt, _, tm = imeta.shape
    grid_spec = pltpu.PrefetchScalarGridSpec(
        num_scalar_prefetch=1,
        grid=(nt,),
        in_specs=[pl.BlockSpec((1, 8, tm), lambda i, seg: (i, 0, 0), memory_space=pltpu.SMEM),
                  pl.BlockSpec((1, 8, tm), lambda i, seg: (jnp.minimum(i + 1, nt - 1), 0, 0),
                               memory_space=pltpu.SMEM),
                  pl.BlockSpec(memory_space=pl.ANY),
                  pl.BlockSpec((tm, d), lambda i, seg: (i, 0)),
                  pl.BlockSpec((tm, LANES), lambda i, seg: (i, 0))],
        out_specs=pl.BlockSpec((tm, d), lambda i, seg: (i, 0)),
        scratch_shapes=[pltpu.VMEM((2, 2, tm * PACK_ROWS, LANES), U32), pltpu.SemaphoreType.DMA((2,))],
    )
    return pl.pallas_call(
        _combine_body,
        grid_spec=grid_spec,
        out_shape=_sds((t, d), F32),
        compiler_params=_cparams(("arbitrary",)),
        name="combine",
    )(seg_start, imeta, imeta, ys, x1, meta)


def moe_plan(counts, n_tokens, tr=ROW_TILE):
    n_tiles = (2 * n_tokens) // tr + N_EXPERTS
    cnt = counts[0, :N_EXPERTS].astype(I32)
    tiles = (cnt + tr - 1) // tr
    e = jnp.arange(N_EXPERTS, dtype=I32)
    tile_end = jnp.sum(jnp.where(e[:, None] <= e[None, :], tiles[:, None], 0), axis=0)
    seg_start = (tile_end - tiles) * tr
    n_used = tile_end[N_EXPERTS - 1]
    tid = jnp.arange(n_tiles, dtype=I32)
    tile_valid = (tid < n_used).astype(I32)
    last = jnp.maximum(n_used - 1, 0)
    tile_expert = jnp.sum((tile_end[None, :] <= jnp.minimum(tid, last)[:, None]).astype(I32), axis=1)
    tile_expert = jnp.minimum(tile_expert, N_EXPERTS - 1)
    return seg_start, tile_expert, tile_valid, n_tiles * tr


def mixer_c(z, qn, kn, cos_t, sin_t, batch):
    outs, lses, dils = [], [], []
    for g, (window, dil) in enumerate(DIL_CFG):
        assert window // dil == N_BACK
        o, l = att_c(prep_c(z, g, dil, qn, kn, cos_t, sin_t, batch), dil)
        outs.append(o)
        lses.append(l)
        dils.append(dil)
    return merge_c(outs, lses, dils)


def layer(x, p, lam_init, batch, tabs):
    t, d = x.shape
    s = t // batch
    z, h = proj(x, p["norm_mix"], p["w_in"])
    y_a = mix_a(z, p["sg_norm"], p["sg_w"], p["sg_b"])
    y_b = mix_b(z, p["conv_w"], batch)
    y_c = mixer_c(z, p["qn_c"], p["kn_c"], tabs[0], tabs[1], batch)
    q1, q2, kd, vt = prep_d(z, p["qn_d"], p["kn_d"], tabs[2], tabs[3], s)
    y_d = diff_att(q1.reshape(batch, s, -1), q2.reshape(batch, s, -1), kd.reshape(batch, s, -1), vt,
                   p["lam_q1"], p["lam_k1"], p["lam_q2"], p["lam_k2"], p["subln_d"],
                   lam_init).reshape(t, BRANCH_W)
    merged = gate_merge(h, (y_a, y_b, y_c, y_d), p["w_gate"], p["b_gate"], p["w_branch"])
    x1, hp, meta, counts, imeta = outproj_route(merged, x, p["w_out"], p["norm_ffn"], p["w_r"], p["b_r"])
    seg_start, tile_expert, tile_valid, rows = moe_plan(counts, t)
    xs = dispatch(hp, seg_start, imeta, rows)
    ys = experts(xs, tile_expert, tile_valid, p["w_up"], p["w_down"], p["layer"])
    return combine(ys, seg_start, imeta, x1, meta)


def kernel(x, norm_mix, w_in, sg_norm, sg_w, sg_b, conv_w, qn_c, kn_c, qn_d, kn_d, lam_q1, lam_k1, lam_q2,
           lam_k2, subln_d, w_gate, b_gate, w_branch, w_out, norm_ffn, w_rg, b_rg, w_re, b_re, w_up, w_down):
    batch, seq, d = x.shape
    depth = w_in.shape[0]
    tabs = rope_tables(seq, HEAD_DIM // 2) + rope_tables(seq, DIFF_DIM // 2)
    xf = x.reshape(batch * seq, d)
    for l in range(depth):
        pad = LANES - N_EXPERTS - N_GROUPS
        w_r = jnp.concatenate([w_re[l], w_rg[l], jnp.zeros((d, pad), F32)], axis=1).astype(BF16)
        b_r = jnp.concatenate([b_re[l], b_rg[l], jnp.zeros((pad,), F32)]).reshape(1, LANES)
        p = dict(norm_mix=norm_mix[l], w_in=w_in[l].astype(BF16), sg_norm=sg_norm[l], sg_w=sg_w[l], sg_b=sg_b[l],
                 conv_w=conv_w[l], qn_c=qn_c[l], kn_c=kn_c[l], qn_d=qn_d[l], kn_d=kn_d[l], lam_q1=lam_q1[l],
                 lam_k1=lam_k1[l], lam_q2=lam_q2[l], lam_k2=lam_k2[l], subln_d=subln_d[l],
                 w_gate=w_gate[l].astype(BF16), b_gate=b_gate[l], w_branch=w_branch[l].astype(BF16),
                 w_out=w_out[l].astype(BF16), norm_ffn=norm_ffn[l], w_r=w_r, b_r=b_r,
                 w_up=w_up, w_down=w_down, layer=l)
        xf = layer(xf, p, 0.8 - 0.6 * math.exp(-0.3 * l), batch, tabs)
    return xf.reshape(batch, seq, d)
```

```python
import functools
import math

import jax
import jax.numpy as jnp
from jax import lax
from jax.experimental import pallas as pl
from jax.experimental.pallas import tpu as pltpu

F32 = jnp.float32
BF16 = jnp.bfloat16
I32 = jnp.int32
U32 = jnp.uint32

D_MODEL = 2048
BRANCH_W = 512
HEAD_DIM = 128
N_HEADS = 4
SG_CHUNK = 128
DIL_CFG = ((128, 1), (512, 4), (2048, 16))
N_BACK = 128
DIFF_DIM = 64
ROPE_THETA = 10000.0
W_A = 2 * BRANCH_W
W_B = 3 * BRANCH_W
W_C = 9 * BRANCH_W
N_GROUPS = 4
EXP_PER_GROUP = 8
N_EXPERTS = 32
D_EXPERT = 512
EPS = 1e-6
NEG = -1e30

LANES = 128
PACK_ROWS = D_MODEL // 2 // LANES
VMEM_LIMIT = 56 * 1024 * 1024
ROW_TILE = 512
DIFF_TILE = 256
DIFF_KTILE = 2 * DIFF_TILE
ROUTE_TILE = 512
ATTC_ROWS = 512
ZERO_ROWS = 256
ROUTE_SPLIT = 2
DMA_UNROLL = 8


def _cparams(sem, **kw):
    return pltpu.CompilerParams(dimension_semantics=sem, vmem_limit_bytes=VMEM_LIMIT, **kw)


def _sds(shape, dtype):
    return jax.ShapeDtypeStruct(shape, dtype)


def _proj_body(x_ref, g_ref, w_ref, z_ref, h_ref):
    @pl.when(pl.program_id(1) == 0)
    def _():
        x = x_ref[...]
        ms = jnp.mean(x * x, axis=-1, keepdims=True)
        h_ref[...] = (x * lax.rsqrt(ms + EPS) * g_ref[...]).astype(BF16)

    z_ref[...] = jnp.dot(h_ref[...], w_ref[...], preferred_element_type=F32).astype(z_ref.dtype)


def proj(x, g, w, tm=1024, tn=512):
    t, d = x.shape
    n = w.shape[1]
    return pl.pallas_call(
        _proj_body,
        grid=(t // tm, n // tn),
        in_specs=[pl.BlockSpec((tm, d), lambda i, j: (i, 0)),
                  pl.BlockSpec((1, d), lambda i, j: (0, 0)),
                  pl.BlockSpec((d, tn), lambda i, j: (0, j))],
        out_specs=[pl.BlockSpec((tm, tn), lambda i, j: (i, j)),
                   pl.BlockSpec((tm, d), lambda i, j: (i, 0))],
        out_shape=[_sds((t, n), BF16), _sds((t, d), BF16)],
        compiler_params=_cparams(("arbitrary", "arbitrary")),
        name="proj",
    )(x, g.reshape(1, d), w)


def _gelu_tanh(x):
    c = math.sqrt(2.0 / math.pi)
    return 0.5 * x * (1.0 + jnp.tanh(c * (x + 0.044715 * (x * x * x))))


def _mixa_body(z_ref, g_ref, w_ref, b_ref, o_ref):
    tm = z_ref.shape[0]
    ga = _gelu_tanh(z_ref[...].astype(F32))
    u = ga[:, :BRANCH_W]
    v = ga[:, BRANCH_W:]
    ms = jnp.mean(v * v, axis=-1, keepdims=True)
    vn = (v * lax.rsqrt(ms + EPS) * g_ref[...]).astype(BF16)
    row = lax.broadcasted_iota(I32, (SG_CHUNK, SG_CHUNK), 0)
    col = lax.broadcasted_iota(I32, (SG_CHUNK, SG_CHUNK), 1)
    for g in range(N_HEADS):
        wg = jnp.where(row >= col, w_ref[g], 0.0).astype(BF16)
        bias = b_ref[g]
        cs = slice(g * SG_CHUNK, (g + 1) * SG_CHUNK)
        for c in range(tm // SG_CHUNK):
            rs = slice(c * SG_CHUNK, (c + 1) * SG_CHUNK)
            mixed = jnp.dot(wg, vn[rs, cs], preferred_element_type=F32) + bias
            o_ref[rs, cs] = (u[rs, cs] * mixed).astype(o_ref.dtype)


def mix_a(z, sg_norm, sg_w, sg_b, tm=512):
    t = z.shape[0]
    bias = jnp.broadcast_to(sg_b[:, :, None], (N_HEADS, SG_CHUNK, SG_CHUNK)).astype(F32)
    return pl.pallas_call(
        _mixa_body,
        grid=(t // tm,),
        in_specs=[pl.BlockSpec((tm, W_A), lambda i: (i, 0)),
                  pl.BlockSpec((1, BRANCH_W), lambda i: (0, 0)),
                  pl.BlockSpec((N_HEADS, SG_CHUNK, SG_CHUNK), lambda i: (0, 0, 0)),
                  pl.BlockSpec((N_HEADS, SG_CHUNK, SG_CHUNK), lambda i: (0, 0, 0))],
        out_specs=pl.BlockSpec((tm, BRANCH_W), lambda i: (i, 0)),
        out_shape=_sds((t, BRANCH_W), BF16),
        compiler_params=_cparams(("arbitrary",)),
        name="mix_a",
    )(z, sg_norm.reshape(1, BRANCH_W), sg_w, bias)


def _mixb_body(bg_ref, cg_ref, hb_ref, w_ref, o_ref, prev_ref):
    @pl.when(pl.program_id(1) == 0)
    def _():
        prev_ref[...] = jnp.zeros_like(prev_ref)

    zz = cg_ref[...].astype(F32) * hb_ref[...].astype(F32)
    tm = zz.shape[0]
    row = lax.broadcasted_iota(I32, zz.shape, 0)
    p = prev_ref[...]
    z1 = jnp.where(row == 0, p[7:8, :], pltpu.roll(zz, 1, 0))
    z2 = jnp.where(row == 0, p[6:7, :], jnp.where(row == 1, p[7:8, :], pltpu.roll(zz, 2, 0)))
    w = w_ref[...]
    y = w[0:1, :] * z2 + w[1:2, :] * z1 + w[2:3, :] * zz
    o_ref[...] = (bg_ref[...].astype(F32) * y).astype(o_ref.dtype)
    prev_ref[...] = zz[tm - 8:, :]


def mix_b(z, conv_w, batch, tm=512):
    t = z.shape[0]
    nt = t // batch // tm
    c0 = W_A // BRANCH_W

    def zspec(k):
        return pl.BlockSpec((tm, BRANCH_W), lambda b, i: (b * nt + i, c0 + k))

    return pl.pallas_call(
        _mixb_body,
        grid=(batch, nt),
        in_specs=[zspec(0), zspec(1), zspec(2), pl.BlockSpec((3, BRANCH_W), lambda b, i: (0, 0))],
        out_specs=pl.BlockSpec((tm, BRANCH_W), lambda b, i: (b * nt + i, 0)),
        out_shape=_sds((t, BRANCH_W), BF16),
        scratch_shapes=[pltpu.VMEM((8, BRANCH_W), F32)],
        compiler_params=_cparams(("arbitrary", "arbitrary")),
        name="mix_b",
    )(z, z, z, conv_w)


def rope_tables(s, half):
    inv = ROPE_THETA ** (-jnp.arange(half, dtype=F32) / half)
    ang = jnp.arange(s, dtype=F32)[:, None] * inv[None, :]
    cos, sin = jnp.cos(ang), jnp.sin(ang)
    reps = LANES // (2 * half)
    cos_t = jnp.tile(jnp.concatenate([cos, cos], axis=-1), (1, reps))
    sin_t = jnp.tile(jnp.concatenate([-sin, sin], axis=-1), (1, reps))
    return cos_t, sin_t


def _group_matrices(width):
    r = lax.broadcasted_iota(I32, (LANES, LANES), 0)
    c = lax.broadcasted_iota(I32, (LANES, LANES), 1)
    partner = (c // width) * width + (c % width + width // 2) % width
    ones = jnp.where((r // width) == (c // width), 1.0, 0.0).astype(BF16)
    perm = jnp.where(r == partner, 1.0, 0.0).astype(BF16)
    return ones, perm


def _norm_rope(xh, gain, cos, sin, ones, perm, width):
    ssq = jnp.dot((xh * xh).astype(BF16), ones, preferred_element_type=F32)
    xn = xh * lax.rsqrt(ssq * (1.0 / width) + EPS) * gain
    rot = jnp.dot(xn.astype(BF16), perm, preferred_element_type=F32)
    return xn * cos + rot * sin


def _prep_c_body(dil, zq_ref, zk_ref, zv_ref, gq_ref, gk_ref, cos_ref, sin_ref, o_ref, scr_ref):
    tm = zq_ref.shape[0]
    cos = cos_ref[...]
    sin = sin_ref[...]

    ones, perm = _group_matrices(HEAD_DIM)

    def norm_rope(xh, gain):
        return _norm_rope(xh, gain, cos, sin, ones, perm, HEAD_DIM)

    width = 3 * BRANCH_W
    for w, ref in enumerate((zq_ref, zk_ref, zv_ref)):
        for h in range(N_HEADS):
            xh = ref[:, h * HEAD_DIM:(h + 1) * HEAD_DIM].astype(F32)
            val = xh if w == 2 else norm_rope(xh, (gq_ref, gk_ref)[w][...])
            c = w * BRANCH_W + h * HEAD_DIM
            if dil == 1:
                o_ref[0, :, c:c + HEAD_DIM] = val.astype(o_ref.dtype)
            else:
                buf = scr_ref.at[w * N_HEADS + h]
                buf[...] = val
                for r in range(dil):
                    o_ref[0, :, r * width + c:r * width + c + HEAD_DIM] = (
                        buf[pl.ds(r, tm // dil, stride=dil), :].astype(o_ref.dtype))


def prep_c(z, g, dil, qn, kn, cos_t, sin_t, batch, tm=1024):
    t = z.shape[0]
    seq = t // batch
    ns = seq // tm
    c0 = (W_A + W_B) // BRANCH_W + g
    gq = (qn * (HEAD_DIM ** -0.5 * math.log2(math.e))).reshape(1, HEAD_DIM).astype(F32)
    gk = kn.reshape(1, HEAD_DIM).astype(F32)
    vec = pl.BlockSpec((1, HEAD_DIM), lambda i: (0, 0))
    tab = pl.BlockSpec((tm, LANES), lambda i: (i % ns, 0))
    return pl.pallas_call(
        functools.partial(_prep_c_body, dil),
        grid=(t // tm,),
        in_specs=[pl.BlockSpec((tm, BRANCH_W), lambda i: (i, c0)),
                  pl.BlockSpec((tm, BRANCH_W), lambda i: (i, c0 + 3)),
                  pl.BlockSpec((tm, BRANCH_W), lambda i: (i, c0 + 6)), vec, vec, tab, tab],
        out_specs=pl.BlockSpec((1, tm // dil, dil * 3 * BRANCH_W), lambda i: (i // ns, i % ns, 0)),
        out_shape=_sds((batch, seq // dil, dil * 3 * BRANCH_W), BF16),
        scratch_shapes=[pltpu.VMEM((3 * N_HEADS, tm, HEAD_DIM), F32)],
        compiler_params=_cparams(("arbitrary",)),
        name="prep_c",
    )(z, z, z, gq, gk, cos_t, sin_t)


def _attc_body(cur_ref, prev_ref, o_ref, lse_ref):
    i = pl.program_id(1)
    nsub = cur_ref.shape[1] // N_BACK
    ncls = cur_ref.shape[2] // (3 * BRANCH_W)
    qi = lax.broadcasted_iota(I32, (N_BACK, N_BACK), 0)
    kj = lax.broadcasted_iota(I32, (N_BACK, N_BACK), 1)
    ok_own = kj <= qi
    ok_before = kj >= qi
    ok_first = (kj + jnp.where(i > 0, 0, -2 * N_BACK)) >= qi
    dn = (((1,), (1,)), ((), ()))
    subs = [slice(u * N_BACK, (u + 1) * N_BACK) for u in range(nsub)]

    def col(c, w, h):
        start = (c * 3 + w) * BRANCH_W + h * HEAD_DIM
        return slice(start, start + HEAD_DIM)

    units = [(c, h, u) for c in range(ncls) for h in range(N_HEADS) for u in range(nsub)]
    scores = []
    for c, h, u in units:
        q = cur_ref[0, subs[u], col(c, 0, h)]
        k_before = prev_ref[0, :, col(c, 1, h)] if u == 0 else cur_ref[0, subs[u - 1], col(c, 1, h)]
        s_before = lax.dot_general(q, k_before, dn, preferred_element_type=F32)
        s_own = lax.dot_general(q, cur_ref[0, subs[u], col(c, 1, h)], dn, preferred_element_type=F32)
        scores.append((jnp.where(ok_first if u == 0 else ok_before, s_before, NEG),
                       jnp.where(ok_own, s_own, NEG)))
    probs = []
    for s_before, s_own in scores:
        m = jnp.max(jnp.maximum(s_before, s_own), axis=-1, keepdims=True)
        probs.append((jnp.exp2(s_before - m).astype(BF16), jnp.exp2(s_own - m).astype(BF16), m))
    ones = jnp.ones((N_BACK, HEAD_DIM), BF16)
    for (c, h, u), (p_before, p_own, m) in zip(units, probs):
        v_before = prev_ref[0, :, col(c, 2, h)] if u == 0 else cur_ref[0, subs[u - 1], col(c, 2, h)]
        ol = (jnp.dot(p_before, jnp.concatenate([v_before, ones], axis=1), preferred_element_type=F32)
              + jnp.dot(p_own, jnp.concatenate([cur_ref[0, subs[u], col(c, 2, h)], ones], axis=1),
                        preferred_element_type=F32))
        l = ol[:, HEAD_DIM:]
        out = slice(c * BRANCH_W + h * HEAD_DIM, c * BRANCH_W + (h + 1) * HEAD_DIM)
        o_ref[0, subs[u], out] = (ol[:, :HEAD_DIM] / l).astype(o_ref.dtype)
        lse_ref[0, subs[u], out] = m + jnp.log2(l)


def att_c(qkv, dil):
    b, length, _ = qkv.shape
    tq = min(ATTC_ROWS, length)
    ncls = min(dil, ATTC_ROWS // tq)
    r = tq // N_BACK
    groups = dil // ncls
    width = ncls * 3 * BRANCH_W
    out = pl.BlockSpec((1, tq, ncls * BRANCH_W), lambda n, i: (n // groups, i, n % groups))
    return pl.pallas_call(
        _attc_body,
        grid=(b * groups, length // tq),
        in_specs=[pl.BlockSpec((1, tq, width), lambda n, i: (n // groups, i, n % groups)),
                  pl.BlockSpec((1, N_BACK, width), lambda n, i: (n // groups, jnp.maximum(i * r - 1, 0), n % groups))],
        out_specs=[out, out],
        out_shape=[_sds((b, length, dil * BRANCH_W), BF16), _sds((b, length, dil * BRANCH_W), F32)],
        compiler_params=_cparams(("arbitrary", "arbitrary")),
        name="att_c",
    )(qkv, qkv)


def _merge_c_body(dils, *refs):
    n = len(dils)
    o_refs, l_refs, y_ref, scr = refs[:n], refs[n:2 * n], refs[2 * n], refs[2 * n + 1:]
    tm = y_ref.shape[0]

    def natural(ref, dil, buf):
        if dil == 1:
            return ref[0].astype(F32)
        for h in range(N_HEADS):
            for r in range(dil):
                c = r * BRANCH_W + h * HEAD_DIM
                buf[h, pl.ds(r, tm // dil, stride=dil), :] = ref[0, :, c:c + HEAD_DIM].astype(F32)
        return jnp.concatenate([buf[h] for h in range(N_HEADS)], axis=-1)

    outs = [natural(o_refs[g], dils[g], scr[2 * g]) for g in range(n)]
    lses = [natural(l_refs[g], dils[g], scr[2 * g + 1]) for g in range(n)]
    m = functools.reduce(jnp.maximum, lses)
    es = [jnp.exp2(l - m) for l in lses]
    y = functools.reduce(lambda a, b: a + b, [e * o for e, o in zip(es, outs)])
    y_ref[...] = (y / functools.reduce(lambda a, b: a + b, es)).astype(y_ref.dtype)


def merge_c(outs, lses, dils, tm=512):
    batch, seq = outs[0].shape[0], outs[0].shape[1] * dils[0]
    t = batch * seq
    ns = seq // tm
    specs = [pl.BlockSpec((1, tm // d, d * BRANCH_W), lambda i: (i // ns, i % ns, 0)) for d in dils]
    return pl.pallas_call(
        functools.partial(_merge_c_body, tuple(dils)),
        grid=(t // tm,),
        in_specs=specs + specs,
        out_specs=pl.BlockSpec((tm, BRANCH_W), lambda i: (i, 0)),
        out_shape=_sds((t, BRANCH_W), BF16),
        scratch_shapes=[pltpu.VMEM((N_HEADS, tm, HEAD_DIM), F32) for _ in range(2 * len(dils))],
        compiler_params=_cparams(("arbitrary",)),
        name="merge_c",
    )(*outs, *lses)


def _prep_d_body(zq_ref, zk_ref, zv_ref, gq_ref, gk_ref, cos_ref, sin_ref, q1_ref, q2_ref, k_ref, vt_ref):
    tm = zq_ref.shape[0]
    lane = lax.broadcasted_iota(I32, (tm, LANES), 1)
    lo = lane < DIFF_DIM
    cos = cos_ref[...]
    sin = sin_ref[...]
    zq = zq_ref[...].astype(F32)
    zk = zk_ref[...].astype(F32)
    ones, perm = _group_matrices(DIFF_DIM)
    for h in range(N_HEADS):
        sl = slice(h * HEAD_DIM, (h + 1) * HEAD_DIM)
        q = _norm_rope(zq[:, sl], gq_ref[...], cos, sin, ones, perm, DIFF_DIM)
        q1_ref[:, sl] = jnp.where(lo, q, 0.0).astype(BF16)
        q2_ref[:, sl] = jnp.where(lo, 0.0, q).astype(BF16)
        k_ref[:, sl] = _norm_rope(zk[:, sl], gk_ref[...], cos, sin, ones, perm, DIFF_DIM).astype(BF16)
    vt_ref[0] = zv_ref[...].astype(F32).T.astype(BF16)


def prep_d(z, qn, kn, cos_t, sin_t, seq, tm=DIFF_KTILE):
    t = z.shape[0]
    ns = seq // tm
    c0 = (W_A + W_B + W_C) // BRANCH_W
    gq = jnp.tile(qn * (DIFF_DIM ** -0.5 * math.log2(math.e)), 2).reshape(1, LANES).astype(F32)
    gk = jnp.tile(kn, 2).reshape(1, LANES).astype(F32)
    row = pl.BlockSpec((tm, BRANCH_W), lambda i: (i, 0))
    tab = pl.BlockSpec((tm, LANES), lambda i: (i % ns, 0))
    vec = pl.BlockSpec((1, LANES), lambda i: (0, 0))
    return pl.pallas_call(
        _prep_d_body,
        grid=(t // tm,),
        in_specs=[pl.BlockSpec((tm, BRANCH_W), lambda i: (i, c0)),
                  pl.BlockSpec((tm, BRANCH_W), lambda i: (i, c0 + 1)),
                  pl.BlockSpec((tm, BRANCH_W), lambda i: (i, c0 + 2)), vec, vec, tab, tab],
        out_specs=[row, row, row, pl.BlockSpec((1, BRANCH_W, tm), lambda i: (i, 0, 0))],
        out_shape=[_sds((t, BRANCH_W), BF16)] * 3 + [_sds((t // tm, BRANCH_W, tm), BF16)],
        compiler_params=_cparams(("arbitrary",)),
        name="prep_d",
    )(z, z, z, gq, gk, cos_t, sin_t)


def _diff_body(lam_init, q1_ref, q2_ref, k_ref, vt_ref, lq1, lk1, lq2, lk2, sg_ref, o_ref, acc_ref):
    i = pl.program_id(1)
    tq = q1_ref.shape[1]
    n_chain = 2 * N_HEADS
    lam = (jnp.exp(jnp.sum(lq1[...] * lk1[...], axis=-1, keepdims=True))
           - jnp.exp(jnp.sum(lq2[...] * lk2[...], axis=-1, keepdims=True)) + lam_init)
    tk = vt_ref.shape[2]
    dn = (((1,), (1,)), ((), ()))
    acc_ref[...] = jnp.zeros_like(acc_ref)
    heads = [slice(h * HEAD_DIM, (h + 1) * HEAD_DIM) for h in range(N_HEADS)]

    def step(j, carry, n_keys, shift):
        ms, ls = carry
        start = pl.multiple_of(j * tk, tk)
        scores = []
        for c in range(n_chain):
            q = (q1_ref, q2_ref)[c % 2][0, :, heads[c // 2]]
            scores.append(lax.dot_general(k_ref[0, pl.ds(start, n_keys), heads[c // 2]], q, dn,
                                          preferred_element_type=F32))
        if shift is not None:
            krow = lax.broadcasted_iota(I32, (n_keys, tq), 0)
            qcol = lax.broadcasted_iota(I32, (n_keys, tq), 1)
            visible = krow <= qcol + shift
        new_m, new_l, alphas, probs = [], [], [], []
        for c in range(n_chain):
            s = scores[c] if shift is None else jnp.where(visible, scores[c], NEG)
            m_new = jnp.maximum(ms[c], jnp.max(s, axis=0, keepdims=True))
            a = jnp.exp2(ms[c] - m_new)
            p = jnp.exp2(s - m_new)
            new_l.append(a * ls[c] + jnp.sum(p, axis=0, keepdims=True))
            new_m.append(m_new)
            alphas.append(a)
            probs.append(p.astype(BF16))
        for c in range(n_chain):
            pv = jnp.dot(vt_ref[j, heads[c // 2], 0:n_keys], probs[c], preferred_element_type=F32)
            acc_ref[c] = alphas[c] * acc_ref[c] + pv
        return tuple(new_m), tuple(new_l)

    init = (tuple(jnp.full((1, tq), NEG, F32) for _ in range(n_chain)),
            tuple(jnp.zeros((1, tq), F32) for _ in range(n_chain)))
    n_full = i // 2
    carry = lax.fori_loop(0, n_full, functools.partial(step, n_keys=tk, shift=None), init)
    _, ls = step(n_full, carry, tk, (i % 2) * tq)
    for h in range(N_HEADS):
        o1 = acc_ref[2 * h] * (1.0 / ls[2 * h])
        o2 = acc_ref[2 * h + 1] * (1.0 / ls[2 * h + 1])
        a = o1 - lam * o2
        ms = jnp.mean(a * a, axis=0, keepdims=True)
        y = a * lax.rsqrt(ms + EPS) * sg_ref[...]
        o_ref[0, :, h * HEAD_DIM:(h + 1) * HEAD_DIM] = y.T.astype(o_ref.dtype)


def diff_att(q1, q2, k, vt, lq1, lk1, lq2, lk2, sub_g, lam_init, tq=DIFF_TILE):
    b, s, _ = q1.shape
    tk = vt.shape[2]
    assert tk == 2 * tq
    nk = s // tq
    qs = pl.BlockSpec((1, tq, BRANCH_W), lambda n, i: (n, i, 0))
    vec = pl.BlockSpec((1, DIFF_DIM), lambda n, i: (0, 0))
    sg = jnp.broadcast_to((sub_g * (1.0 - lam_init))[:, None], (HEAD_DIM, tq)).astype(F32)
    return pl.pallas_call(
        functools.partial(_diff_body, lam_init),
        grid=(b, nk),
        in_specs=[qs, qs,
                  pl.BlockSpec((1, s, BRANCH_W), lambda n, i: (n, 0, 0)),
                  pl.BlockSpec((s // tk, BRANCH_W, tk), lambda n, i: (n, 0, 0)),
                  vec, vec, vec, vec,
                  pl.BlockSpec((HEAD_DIM, tq), lambda n, i: (0, 0))],
        out_specs=qs,
        out_shape=_sds((b, s, BRANCH_W), BF16),
        scratch_shapes=[pltpu.VMEM((2 * N_HEADS, HEAD_DIM, tq), F32)],
        compiler_params=_cparams(("arbitrary", "arbitrary")),
        name="diff_att",
    )(q1, q2, k, vt, lq1.reshape(1, -1), lk1.reshape(1, -1), lq2.reshape(1, -1), lk2.reshape(1, -1), sg)


def _gate_body(h_ref, ya, yb, yc, yd, wg_ref, bg_ref, wb_ref, o_ref):
    h = h_ref[...]
    acc = None
    for i, y in enumerate((ya, yb, yc, yd)):
        g = jax.nn.sigmoid(jnp.dot(h, wg_ref[i], preferred_element_type=F32) + bg_ref[i:i + 1, :])
        t = jnp.dot(y[...], wb_ref[i], preferred_element_type=F32)
        acc = g * t if acc is None else acc + g * t
    o_ref[...] = acc.astype(o_ref.dtype)


def gate_merge(h, ys, w_gate, b_gate, w_branch, tm=512, tn=512):
    t, d = h.shape
    ysp = pl.BlockSpec((tm, BRANCH_W), lambda j, i: (i, 0))
    return pl.pallas_call(
        _gate_body,
        grid=(d // tn, t // tm),
        in_specs=[pl.BlockSpec((tm, d), lambda j, i: (i, 0)), ysp, ysp, ysp, ysp,
                  pl.BlockSpec((4, d, tn), lambda j, i: (0, 0, j)),
                  pl.BlockSpec((4, tn), lambda j, i: (0, j)),
                  pl.BlockSpec((4, BRANCH_W, tn), lambda j, i: (0, 0, j))],
        out_specs=pl.BlockSpec((tm, tn), lambda j, i: (i, j)),
        out_shape=_sds((t, d), BF16),
        compiler_params=_cparams(("arbitrary", "arbitrary")),
        name="gate_merge",
    )(h, *ys, w_gate, b_gate, w_branch)


def _pack_rows(x):
    half = x.shape[1] // 2
    lo = pltpu.bitcast(x[:, :half].astype(BF16).astype(F32), U32)
    hi = pltpu.bitcast(x[:, half:].astype(BF16).astype(F32), U32)
    return (lo >> 16) | hi


def _store_packed(ref, words):
    m = words.shape[0]
    for c in range(PACK_ROWS):
        ref[pl.ds(c, m, stride=PACK_ROWS), :] = words[:, c * LANES:(c + 1) * LANES]


def _load_packed(ref, m):
    words = jnp.concatenate([ref[pl.ds(c, m, stride=PACK_ROWS), :] for c in range(PACK_ROWS)], axis=-1)
    lo = pltpu.bitcast(words << 16, F32)
    hi = pltpu.bitcast(words & jnp.uint32(0xFFFF0000), F32)
    return lo, hi


def _outproj_body(m_ref, x_ref, wo_ref, g_ref, wr_ref, br_ref, x1_ref, hp_ref, meta_ref, cnt_ref, imeta_ref,
                  run_ref):
    @pl.when(pl.program_id(0) == 0)
    def _():
        run_ref[...] = jnp.zeros_like(run_ref)

    sub = x_ref.shape[0] // ROUTE_SPLIT
    x1s = []
    for s in range(ROUTE_SPLIT):
        rows = slice(s * sub, (s + 1) * sub)
        x1 = x_ref[rows, :] + jnp.dot(m_ref[rows, :], wo_ref[...], preferred_element_type=F32)
        x1_ref[rows, :] = x1
        x1s.append(x1)
    run = run_ref[0:1, :]
    for s in range(ROUTE_SPLIT):
        rows = slice(s * sub, (s + 1) * sub)
        ms = jnp.mean(x1s[s] * x1s[s], axis=-1, keepdims=True)
        h2 = x1s[s] * lax.rsqrt(ms + EPS) * g_ref[...]
        _store_packed(hp_ref.at[pl.ds(s * sub * PACK_ROWS, sub * PACK_ROWS), :], _pack_rows(h2))
        meta, run = _route(h2, wr_ref, br_ref, run)
        meta_ref[rows, :] = meta
        imeta_ref[0, :, rows] = meta.T[0:8, :].astype(I32)
    run_ref[...] = jnp.broadcast_to(run, run_ref.shape)
    cnt_ref[...] = jnp.broadcast_to(run, cnt_ref.shape)


def _route(h2, wr_ref, br_ref, run):
    tm = h2.shape[0]
    lg = jnp.dot(h2.astype(BF16), wr_ref[...], preferred_element_type=F32) + br_ref[...]
    lane = lax.broadcasted_iota(I32, (tm, LANES), 1)
    big = jnp.int32(4 * LANES)
    is_g = (lane >= N_EXPERTS) & (lane < N_EXPERTS + N_GROUPS)
    gl = jnp.where(is_g, lg, NEG)
    gm = jnp.max(gl, axis=-1, keepdims=True)
    g_p = 1.0 / jnp.sum(jnp.exp(gl - gm), axis=-1, keepdims=True)
    gidx = jnp.min(jnp.where(gl == gm, lane, big), axis=-1, keepdims=True) - N_EXPERTS
    lo_e = gidx * EXP_PER_GROUP
    in_g = (lane >= lo_e) & (lane < lo_e + EXP_PER_GROUP)
    el = jnp.where(in_g, lg, NEG)
    v1 = jnp.max(el, axis=-1, keepdims=True)
    i1 = jnp.min(jnp.where(el == v1, lane, big), axis=-1, keepdims=True)
    el2 = jnp.where(lane == i1, NEG, el)
    v2 = jnp.max(el2, axis=-1, keepdims=True)
    i2 = jnp.min(jnp.where(el2 == v2, lane, big), axis=-1, keepdims=True)
    e2 = jnp.exp(v2 - v1)
    w1 = g_p / (1.0 + e2)
    w2 = g_p * e2 / (1.0 + e2)

    oh1 = lane == i1
    oh2 = lane == i2
    a = jnp.where(oh1, 1.0, jnp.where(oh2, 1.0, 0.0))
    r = lax.broadcasted_iota(I32, (tm, tm), 0)
    c = lax.broadcasted_iota(I32, (tm, tm), 1)
    before = jnp.where(c < r, 1.0, 0.0).astype(BF16)
    pref = jnp.dot(before, a.astype(BF16), preferred_element_type=F32) + run
    rank1 = jnp.sum(jnp.where(oh1, pref, 0.0), axis=-1, keepdims=True)
    rank2 = jnp.sum(jnp.where(oh2, pref, 0.0), axis=-1, keepdims=True)

    meta = jnp.where(lane == 0, i1.astype(F32), 0.0)
    meta = jnp.where(lane == 1, i2.astype(F32), meta)
    meta = jnp.where(lane == 2, w1, meta)
    meta = jnp.where(lane == 3, w2, meta)
    meta = jnp.where(lane == 4, rank1, meta)
    meta = jnp.where(lane == 5, rank2, meta)
    return meta, run + jnp.sum(a, axis=0, keepdims=True)


def outproj_route(merged, x, w_out, g2, w_r, b_r, tm=ROUTE_TILE):
    t, d = x.shape
    return pl.pallas_call(
        _outproj_body,
        grid=(t // tm,),
        in_specs=[pl.BlockSpec((tm, d), lambda i: (i, 0)),
                  pl.BlockSpec((tm, d), lambda i: (i, 0)),
                  pl.BlockSpec((d, d), lambda i: (0, 0)),
                  pl.BlockSpec((1, d), lambda i: (0, 0)),
                  pl.BlockSpec((d, LANES), lambda i: (0, 0)),
                  pl.BlockSpec((1, LANES), lambda i: (0, 0))],
        out_specs=[pl.BlockSpec((tm, d), lambda i: (i, 0)),
                   pl.BlockSpec((tm * PACK_ROWS, LANES), lambda i: (i, 0)),
                   pl.BlockSpec((tm, LANES), lambda i: (i, 0)),
                   pl.BlockSpec((8, LANES), lambda i: (0, 0)),
                   pl.BlockSpec((1, 8, tm), lambda i: (i, 0, 0))],
        out_shape=[_sds((t, d), F32), _sds((t * PACK_ROWS, LANES), U32), _sds((t, LANES), F32),
                   _sds((8, LANES), F32), _sds((t // tm, 8, tm), I32)],
        scratch_shapes=[pltpu.VMEM((8, LANES), F32)],
        compiler_params=_cparams(("arbitrary",)),
        name="outproj_route",
    )(merged, x, w_out, g2.reshape(1, d), w_r, b_r)


def _slot(seg_ref, imeta_ref, k, t):
    return seg_ref[imeta_ref[0, k, t]] + imeta_ref[0, 4 + k, t]


def _zero_fill(plan_ref, xs_ref, zero_ref, zsem, wait):
    def chunk(first_row, n_rows):
        cp = pltpu.make_async_copy(
            zero_ref.at[pl.ds(0, n_rows * PACK_ROWS), :],
            xs_ref.at[pl.ds(pl.multiple_of(first_row * PACK_ROWS, PACK_ROWS), n_rows * PACK_ROWS), :], zsem)
        if wait:
            cp.wait()
        else:
            cp.start()

    def per_expert(e, carry):
        row = plan_ref[N_EXPERTS + e]
        n = plan_ref[2 * N_EXPERTS + e]
        bit = ZERO_ROWS
        while bit >= 1:
            has = (n & bit) != 0
            pl.when(has)(functools.partial(chunk, row, bit))
            row = row + jnp.where(has, bit, 0)
            bit //= 2
        return carry

    def per_tail(j, carry):
        chunk(plan_ref[3 * N_EXPERTS] + j * ZERO_ROWS, ZERO_ROWS)
        return carry

    lax.fori_loop(0, N_EXPERTS, per_expert, 0)
    lax.fori_loop(0, plan_ref[3 * N_EXPERTS + 1], per_tail, 0)


def _dispatch_body(seg_ref, imeta_ref, hp_ref, xs_ref, zero_ref, sem, zsem):
    tm = hp_ref.shape[0] // PACK_ROWS
    first = pl.program_id(0) == 0

    @pl.when(first)
    def _():
        zero_ref[...] = jnp.zeros_like(zero_ref)
        _zero_fill(seg_ref, xs_ref, zero_ref, zsem, wait=False)

    def row_copy(t, k):
        pos = _slot(seg_ref, imeta_ref, k, t)
        src = hp_ref.at[pl.ds(pl.multiple_of(t * PACK_ROWS, PACK_ROWS), PACK_ROWS), :]
        dst = xs_ref.at[pl.ds(pl.multiple_of(pos * PACK_ROWS, PACK_ROWS), PACK_ROWS), :]
        return pltpu.make_async_copy(src, dst, sem)

    def start(t, carry):
        row_copy(t, 0).start()
        row_copy(t, 1).start()
        return carry

    lax.fori_loop(0, tm, start, 0, unroll=DMA_UNROLL)
    whole = pltpu.make_async_copy(hp_ref, xs_ref.at[pl.ds(0, tm * PACK_ROWS), :], sem)
    whole.wait()
    whole.wait()

    @pl.when(first)
    def _():
        _zero_fill(seg_ref, xs_ref, zero_ref, zsem, wait=True)


def dispatch(hp, plan, imeta, rows):
    nt, _, tm = imeta.shape
    grid_spec = pltpu.PrefetchScalarGridSpec(
        num_scalar_prefetch=1,
        grid=(nt,),
        in_specs=[pl.BlockSpec((1, 8, tm), lambda i, seg: (i, 0, 0), memory_space=pltpu.SMEM),
                  pl.BlockSpec((tm * PACK_ROWS, LANES), lambda i, seg: (i, 0))],
        out_specs=pl.BlockSpec(memory_space=pl.ANY),
        scratch_shapes=[pltpu.VMEM((ZERO_ROWS * PACK_ROWS, LANES), U32), pltpu.SemaphoreType.DMA(()),
                        pltpu.SemaphoreType.DMA(())],
    )
    return pl.pallas_call(
        _dispatch_body,
        grid_spec=grid_spec,
        out_shape=_sds((rows * PACK_ROWS, LANES), U32),
        compiler_params=_cparams(("arbitrary",), has_side_effects=True),
        name="dispatch",
    )(plan, imeta, hp)


def _expert_body(te_ref, tv_ref, xs_ref, wu_ref, wd_ref, ys_ref, wu_bf, wd_bf):
    i = pl.program_id(0)
    tr = xs_ref.shape[0] // PACK_ROWS

    @pl.when(jnp.logical_or(i == 0, te_ref[i] != te_ref[jnp.maximum(i - 1, 0)]))
    def _():
        wu_bf[...] = wu_ref[0, 0].astype(BF16)
        wd_bf[...] = wd_ref[0, 0].astype(BF16)

    @pl.when(tv_ref[i] == 0)
    def _():
        ys_ref[...] = jnp.zeros_like(ys_ref)

    @pl.when(tv_ref[i] > 0)
    def _():
        lo, hi = _load_packed(xs_ref, tr)
        x = jnp.concatenate([lo, hi], axis=-1).astype(BF16)
        ac = jnp.dot(x, wu_bf[...], preferred_element_type=F32)
        a = ac[:, :D_EXPERT]
        c = ac[:, D_EXPERT:]
        hmid = (a * jax.nn.sigmoid(a) * c).astype(BF16)
        y = jnp.dot(hmid, wd_bf[...], preferred_element_type=F32)
        _store_packed(ys_ref, _pack_rows(y))


def experts(xs, tile_expert, tile_valid, w_up, w_down, layer_idx, tr=ROW_TILE):
    rows = xs.shape[0] // PACK_ROWS
    nt = rows // tr
    grid_spec = pltpu.PrefetchScalarGridSpec(
        num_scalar_prefetch=2,
        grid=(nt,),
        in_specs=[pl.BlockSpec((tr * PACK_ROWS, LANES), lambda i, te, tv: (i, 0)),
                  pl.BlockSpec((1, 1, D_MODEL, 2 * D_EXPERT), lambda i, te, tv: (layer_idx, te[i], 0, 0)),
                  pl.BlockSpec((1, 1, D_EXPERT, D_MODEL), lambda i, te, tv: (layer_idx, te[i], 0, 0))],
        out_specs=pl.BlockSpec((tr * PACK_ROWS, LANES), lambda i, te, tv: (i, 0)),
        scratch_shapes=[pltpu.VMEM((D_MODEL, 2 * D_EXPERT), BF16), pltpu.VMEM((D_EXPERT, D_MODEL), BF16)],
    )
    return pl.pallas_call(
        _expert_body,
        grid_spec=grid_spec,
        out_shape=_sds((rows * PACK_ROWS, LANES), U32),
        compiler_params=_cparams(("arbitrary",)),
        name="experts",
    )(tile_expert, tile_valid, xs, w_up, w_down)


def _combine_body(seg_ref, imeta_ref, imeta_next_ref, ys_ref, x_ref, meta_ref, o_ref, buf, sem):
    i = pl.program_id(0)
    tm = x_ref.shape[0]
    slot = i % 2

    def gather(im_ref, dst_slot):
        def start(t, carry):
            for k in range(2):
                pos = _slot(seg_ref, im_ref, k, t)
                src = ys_ref.at[pl.ds(pl.multiple_of(pos * PACK_ROWS, PACK_ROWS), PACK_ROWS), :]
                dst = buf.at[dst_slot, k, pl.ds(pl.multiple_of(t * PACK_ROWS, PACK_ROWS), PACK_ROWS), :]
                pltpu.make_async_copy(src, dst, sem.at[dst_slot]).start()
            return carry

        lax.fori_loop(0, tm, start, 0, unroll=DMA_UNROLL)

    @pl.when(i == 0)
    def _():
        gather(imeta_ref, 0)

    @pl.when(i + 1 < pl.num_programs(0))
    def _():
        gather(imeta_next_ref, 1 - slot)

    for k in range(2):
        pltpu.make_async_copy(ys_ref.at[pl.ds(0, tm * PACK_ROWS), :], buf.at[slot, k], sem.at[slot]).wait()
    meta = meta_ref[...]
    w1 = meta[:, 2:3]
    w2 = meta[:, 3:4]
    half = D_MODEL // 2
    lo1, hi1 = _load_packed(buf.at[slot, 0], tm)
    lo2, hi2 = _load_packed(buf.at[slot, 1], tm)
    o_ref[:, :half] = x_ref[:, :half] + w1 * lo1 + w2 * lo2
    o_ref[:, half:] = x_ref[:, half:] + w1 * hi1 + w2 * hi2


def combine(ys, seg_start, imeta, x1, meta):
    t, d = x1.shape
    nt, _, tm = imeta.shape
    grid_spec = pltpu.PrefetchScalarGridSpec(
        num_scalar_prefetch=1,
        grid=(nt,),
        in_specs=[pl.BlockSpec((1, 8, tm), lambda i, seg: (i, 0, 0), memory_space=pltpu.SMEM),
                  pl.BlockSpec((1, 8, tm), lambda i, seg: (jnp.minimum(i + 1, nt - 1), 0, 0),
                               memory_space=pltpu.SMEM),
                  pl.BlockSpec(memory_space=pl.ANY),
                  pl.BlockSpec((tm, d), lambda i, seg: (i, 0)),
                  pl.BlockSpec((tm, LANES), lambda i, seg: (i, 0))],
        out_specs=pl.BlockSpec((tm, d), lambda i, seg: (i, 0)),
        scratch_shapes=[pltpu.VMEM((2, 2, tm * PACK_ROWS, LANES), U32), pltpu.SemaphoreType.DMA((2,))],
    )
    return pl.pallas_call(
        _combine_body,
        grid_spec=grid_spec,
        out_shape=_sds((t, d), F32),
        compiler_params=_cparams(("arbitrary",)),
        name="combine",
    )(seg_start, imeta, imeta, ys, x1, meta)


def moe_plan(counts, n_tokens, tr=ROW_TILE):
    n_tiles = (2 * n_tokens) // tr + N_EXPERTS
    cnt = counts[0, :N_EXPERTS].astype(I32)
    tiles = (cnt + tr - 1) // tr
    e = jnp.arange(N_EXPERTS, dtype=I32)
    tile_end = jnp.sum(jnp.where(e[:, None] <= e[None, :], tiles[:, None], 0), axis=0)
    seg_start = (tile_end - tiles) * tr
    n_used = tile_end[N_EXPERTS - 1]
    tid = jnp.arange(n_tiles, dtype=I32)
    tile_valid = (tid < n_used).astype(I32)
    last = jnp.maximum(n_used - 1, 0)
    tile_expert = jnp.sum((tile_end[None, :] <= jnp.minimum(tid, last)[:, None]).astype(I32), axis=1)
    tile_expert = jnp.minimum(tile_expert, N_EXPERTS - 1)
    tail = jnp.stack([n_used * tr, (n_tiles - n_used) * (tr // ZERO_ROWS)])
    plan = jnp.concatenate([seg_start, seg_start + cnt, tiles * tr - cnt, tail]).astype(I32)
    return plan, tile_expert, tile_valid, n_tiles * tr


def mixer_c(z, qn, kn, cos_t, sin_t, batch):
    outs, lses, dils = [], [], []
    for g, (window, dil) in enumerate(DIL_CFG):
        assert window // dil == N_BACK
        o, l = att_c(prep_c(z, g, dil, qn, kn, cos_t, sin_t, batch), dil)
        outs.append(o)
        lses.append(l)
        dils.append(dil)
    return merge_c(outs, lses, dils)


def layer(x, p, lam_init, batch, tabs):
    t, d = x.shape
    s = t // batch
    z, h = proj(x, p["norm_mix"], p["w_in"])
    y_a = mix_a(z, p["sg_norm"], p["sg_w"], p["sg_b"])
    y_b = mix_b(z, p["conv_w"], batch)
    y_c = mixer_c(z, p["qn_c"], p["kn_c"], tabs[0], tabs[1], batch)
    q1, q2, kd, vt = prep_d(z, p["qn_d"], p["kn_d"], tabs[2], tabs[3], s)
    y_d = diff_att(q1.reshape(batch, s, -1), q2.reshape(batch, s, -1), kd.reshape(batch, s, -1), vt,
                   p["lam_q1"], p["lam_k1"], p["lam_q2"], p["lam_k2"], p["subln_d"],
                   lam_init).reshape(t, BRANCH_W)
    merged = gate_merge(h, (y_a, y_b, y_c, y_d), p["w_gate"], p["b_gate"], p["w_branch"])
    x1, hp, meta, counts, imeta = outproj_route(merged, x, p["w_out"], p["norm_ffn"], p["w_r"], p["b_r"])
    seg_start, tile_expert, tile_valid, rows = moe_plan(counts, t)
    xs = dispatch(hp, seg_start, imeta, rows)
    ys = experts(xs, tile_expert, tile_valid, p["w_up"], p["w_down"], p["layer"])
    return combine(ys, seg_start, imeta, x1, meta)


def kernel(x, norm_mix, w_in, sg_norm, sg_w, sg_b, conv_w, qn_c, kn_c, qn_d, kn_d, lam_q1, lam_k1, lam_q2,
           lam_k2, subln_d, w_gate, b_gate, w_branch, w_out, norm_ffn, w_rg, b_rg, w_re, b_re, w_up, w_down):
    batch, seq, d = x.shape
    depth = w_in.shape[0]
    tabs = rope_tables(seq, HEAD_DIM // 2) + rope_tables(seq, DIFF_DIM // 2)
    xf = x.reshape(batch * seq, d)
    for l in range(depth):
        pad = LANES - N_EXPERTS - N_GROUPS
        w_r = jnp.concatenate([w_re[l], w_rg[l], jnp.zeros((d, pad), F32)], axis=1).astype(BF16)
        b_r = jnp.concatenate([b_re[l], b_rg[l], jnp.zeros((pad,), F32)]).reshape(1, LANES)
        p = dict(norm_mix=norm_mix[l], w_in=w_in[l].astype(BF16), sg_norm=sg_norm[l], sg_w=sg_w[l], sg_b=sg_b[l],
                 conv_w=conv_w[l], qn_c=qn_c[l], kn_c=kn_c[l], qn_d=qn_d[l], kn_d=kn_d[l], lam_q1=lam_q1[l],
                 lam_k1=lam_k1[l], lam_q2=lam_q2[l], lam_k2=lam_k2[l], subln_d=subln_d[l],
                 w_gate=w_gate[l].astype(BF16), b_gate=b_gate[l], w_branch=w_branch[l].astype(BF16),
                 w_out=w_out[l].astype(BF16), norm_ffn=norm_ffn[l], w_r=w_r, b_r=b_r,
                 w_up=w_up, w_down=w_down, layer=l)
        xf = layer(xf, p, 0.8 - 0.6 * math.exp(-0.3 * l), batch, tabs)
    return xf.reshape(batch, seq, d)
```

```python
import functools
import math

import jax
import jax.numpy as jnp
from jax import lax
from jax.experimental import pallas as pl
from jax.experimental.pallas import tpu as pltpu

F32 = jnp.float32
BF16 = jnp.bfloat16
I32 = jnp.int32

D_MODEL = 2048
BRANCH_W = 512
HEAD_DIM = 128
N_HEADS = 4
SG_CHUNK = 128
DIL_CFG = ((128, 1), (512, 4), (2048, 16))
N_BACK = 128
DIFF_DIM = 64
ROPE_THETA = 10000.0
W_A = 2 * BRANCH_W
W_B = 3 * BRANCH_W
W_C = 9 * BRANCH_W
N_GROUPS = 4
EXP_PER_GROUP = 8
N_EXPERTS = 32
D_EXPERT = 512
EPS = 1e-6
NEG = -1e30

LANES = 128
PACK_ROWS = D_MODEL // LANES
VMEM_LIMIT = 56 * 1024 * 1024
ROW_TILE = 256
DIFF_TILE = 512
DIFF_KTILE = 512
ROUTE_TILE = 512
ATTC_ROWS = 512
ZERO_ROWS = 256
ROUTE_SPLIT = 2
DMA_UNROLL = 8


def _cparams(sem, **kw):
    return pltpu.CompilerParams(dimension_semantics=sem, vmem_limit_bytes=VMEM_LIMIT, **kw)


def _sds(shape, dtype):
    return jax.ShapeDtypeStruct(shape, dtype)


def _proj_body(x_ref, g_ref, w_ref, z_ref, h_ref):
    @pl.when(pl.program_id(1) == 0)
    def _():
        x = x_ref[...]
        ms = jnp.mean(x * x, axis=-1, keepdims=True)
        h_ref[...] = (x * lax.rsqrt(ms + EPS) * g_ref[...]).astype(BF16)

    z_ref[...] = jnp.dot(h_ref[...], w_ref[...], preferred_element_type=F32).astype(z_ref.dtype)


def proj(x, g, w, tm=1024, tn=512):
    t, d = x.shape
    n = w.shape[1]
    return pl.pallas_call(
        _proj_body,
        grid=(t // tm, n // tn),
        in_specs=[pl.BlockSpec((tm, d), lambda i, j: (i, 0)),
                  pl.BlockSpec((1, d), lambda i, j: (0, 0)),
                  pl.BlockSpec((d, tn), lambda i, j: (0, j))],
        out_specs=[pl.BlockSpec((tm, tn), lambda i, j: (i, j)),
                   pl.BlockSpec((tm, d), lambda i, j: (i, 0))],
        out_shape=[_sds((t, n), BF16), _sds((t, d), BF16)],
        compiler_params=_cparams(("arbitrary", "arbitrary")),
        name="proj",
    )(x, g.reshape(1, d), w)


def _gelu_tanh(x):
    c = math.sqrt(2.0 / math.pi)
    return 0.5 * x * (1.0 + jnp.tanh(c * (x + 0.044715 * (x * x * x))))


def _mixa_body(z_ref, g_ref, w_ref, b_ref, o_ref):
    tm = z_ref.shape[0]
    ga = _gelu_tanh(z_ref[...].astype(F32))
    u = ga[:, :BRANCH_W]
    v = ga[:, BRANCH_W:]
    ms = jnp.mean(v * v, axis=-1, keepdims=True)
    vn = (v * lax.rsqrt(ms + EPS) * g_ref[...]).astype(BF16)
    row = lax.broadcasted_iota(I32, (SG_CHUNK, SG_CHUNK), 0)
    col = lax.broadcasted_iota(I32, (SG_CHUNK, SG_CHUNK), 1)
    for g in range(N_HEADS):
        wg = jnp.where(row >= col, w_ref[g], 0.0).astype(BF16)
        bias = b_ref[g]
        cs = slice(g * SG_CHUNK, (g + 1) * SG_CHUNK)
        for c in range(tm // SG_CHUNK):
            rs = slice(c * SG_CHUNK, (c + 1) * SG_CHUNK)
            mixed = jnp.dot(wg, vn[rs, cs], preferred_element_type=F32) + bias
            o_ref[rs, cs] = (u[rs, cs] * mixed).astype(o_ref.dtype)


def mix_a(z, sg_norm, sg_w, sg_b, tm=1024):
    t = z.shape[0]
    bias = jnp.broadcast_to(sg_b[:, :, None], (N_HEADS, SG_CHUNK, SG_CHUNK)).astype(F32)
    return pl.pallas_call(
        _mixa_body,
        grid=(t // tm,),
        in_specs=[pl.BlockSpec((tm, W_A), lambda i: (i, 0)),
                  pl.BlockSpec((1, BRANCH_W), lambda i: (0, 0)),
                  pl.BlockSpec((N_HEADS, SG_CHUNK, SG_CHUNK), lambda i: (0, 0, 0)),
                  pl.BlockSpec((N_HEADS, SG_CHUNK, SG_CHUNK), lambda i: (0, 0, 0))],
        out_specs=pl.BlockSpec((tm, BRANCH_W), lambda i: (i, 0)),
        out_shape=_sds((t, BRANCH_W), BF16),
        compiler_params=_cparams(("arbitrary",)),
        name="mix_a",
    )(z, sg_norm.reshape(1, BRANCH_W), sg_w, bias)


def _mixb_body(bg_ref, cg_ref, hb_ref, w_ref, o_ref, prev_ref):
    @pl.when(pl.program_id(1) == 0)
    def _():
        prev_ref[...] = jnp.zeros_like(prev_ref)

    zz = cg_ref[...].astype(F32) * hb_ref[...].astype(F32)
    tm = zz.shape[0]
    row = lax.broadcasted_iota(I32, zz.shape, 0)
    p = prev_ref[...]
    z1 = jnp.where(row == 0, p[7:8, :], pltpu.roll(zz, 1, 0))
    z2 = jnp.where(row == 0, p[6:7, :], jnp.where(row == 1, p[7:8, :], pltpu.roll(zz, 2, 0)))
    w = w_ref[...]
    y = w[0:1, :] * z2 + w[1:2, :] * z1 + w[2:3, :] * zz
    o_ref[...] = (bg_ref[...].astype(F32) * y).astype(o_ref.dtype)
    prev_ref[...] = zz[tm - 8:, :]


def mix_b(z, conv_w, batch, tm=1024):
    t = z.shape[0]
    nt = t // batch // tm
    c0 = W_A // BRANCH_W

    def zspec(k):
        return pl.BlockSpec((tm, BRANCH_W), lambda b, i: (b * nt + i, c0 + k))

    return pl.pallas_call(
        _mixb_body,
        grid=(batch, nt),
        in_specs=[zspec(0), zspec(1), zspec(2), pl.BlockSpec((3, BRANCH_W), lambda b, i: (0, 0))],
        out_specs=pl.BlockSpec((tm, BRANCH_W), lambda b, i: (b * nt + i, 0)),
        out_shape=_sds((t, BRANCH_W), BF16),
        scratch_shapes=[pltpu.VMEM((8, BRANCH_W), F32)],
        compiler_params=_cparams(("arbitrary", "arbitrary")),
        name="mix_b",
    )(z, z, z, conv_w)


def rope_tables(s, half):
    inv = ROPE_THETA ** (-jnp.arange(half, dtype=F32) / half)
    ang = jnp.arange(s, dtype=F32)[:, None] * inv[None, :]
    cos, sin = jnp.cos(ang), jnp.sin(ang)
    reps = LANES // (2 * half)
    cos_t = jnp.tile(jnp.concatenate([cos, cos], axis=-1), (1, reps))
    sin_t = jnp.tile(jnp.concatenate([-sin, sin], axis=-1), (1, reps))
    return cos_t, sin_t


def _group_matrices(width):
    r = lax.broadcasted_iota(I32, (LANES, LANES), 0)
    c = lax.broadcasted_iota(I32, (LANES, LANES), 1)
    partner = (c // width) * width + (c % width + width // 2) % width
    ones = jnp.where((r // width) == (c // width), 1.0, 0.0).astype(BF16)
    perm = jnp.where(r == partner, 1.0, 0.0).astype(BF16)
    return ones, perm


def _norm_rope(xh, gain, cos, sin, ones, perm, width):
    ssq = jnp.dot((xh * xh).astype(BF16), ones, preferred_element_type=F32)
    xn = xh * lax.rsqrt(ssq * (1.0 / width) + EPS) * gain
    rot = jnp.dot(xn.astype(BF16), perm, preferred_element_type=F32)
    return xn * cos + rot * sin


def _prep_c_body(dil, zq_ref, zk_ref, zv_ref, gq_ref, gk_ref, cos_ref, sin_ref, o_ref, scr_ref):
    tm = zq_ref.shape[0]
    cos = cos_ref[...]
    sin = sin_ref[...]

    ones, perm = _group_matrices(HEAD_DIM)

    def norm_rope(xh, gain):
        return _norm_rope(xh, gain, cos, sin, ones, perm, HEAD_DIM)

    width = 3 * BRANCH_W
    for w, ref in enumerate((zq_ref, zk_ref, zv_ref)):
        for h in range(N_HEADS):
            xh = ref[:, h * HEAD_DIM:(h + 1) * HEAD_DIM].astype(F32)
            val = xh if w == 2 else norm_rope(xh, (gq_ref, gk_ref)[w][...])
            c = w * BRANCH_W + h * HEAD_DIM
            if dil == 1:
                o_ref[0, :, c:c + HEAD_DIM] = val.astype(o_ref.dtype)
            else:
                buf = scr_ref.at[w * N_HEADS + h]
                buf[...] = val
                for r in range(dil):
                    o_ref[0, :, r * width + c:r * width + c + HEAD_DIM] = (
                        buf[pl.ds(r, tm // dil, stride=dil), :].astype(o_ref.dtype))


def prep_c(z, g, dil, qn, kn, cos_t, sin_t, batch, tm=1024):
    t = z.shape[0]
    seq = t // batch
    ns = seq // tm
    c0 = (W_A + W_B) // BRANCH_W + g
    gq = (qn * (HEAD_DIM ** -0.5 * math.log2(math.e))).reshape(1, HEAD_DIM).astype(F32)
    gk = kn.reshape(1, HEAD_DIM).astype(F32)
    vec = pl.BlockSpec((1, HEAD_DIM), lambda i: (0, 0))
    tab = pl.BlockSpec((tm, LANES), lambda i: (i % ns, 0))
    return pl.pallas_call(
        functools.partial(_prep_c_body, dil),
        grid=(t // tm,),
        in_specs=[pl.BlockSpec((tm, BRANCH_W), lambda i: (i, c0)),
                  pl.BlockSpec((tm, BRANCH_W), lambda i: (i, c0 + 3)),
                  pl.BlockSpec((tm, BRANCH_W), lambda i: (i, c0 + 6)), vec, vec, tab, tab],
        out_specs=pl.BlockSpec((1, tm // dil, dil * 3 * BRANCH_W), lambda i: (i // ns, i % ns, 0)),
        out_shape=_sds((batch, seq // dil, dil * 3 * BRANCH_W), BF16),
        scratch_shapes=[pltpu.VMEM((3 * N_HEADS, tm, HEAD_DIM), F32)],
        compiler_params=_cparams(("arbitrary",)),
        name="prep_c",
    )(z, z, z, gq, gk, cos_t, sin_t)


def _attc_body(cur_ref, prev_ref, o_ref, lse_ref):
    i = pl.program_id(1)
    nsub = cur_ref.shape[1] // N_BACK
    ncls = cur_ref.shape[2] // (3 * BRANCH_W)
    qi = lax.broadcasted_iota(I32, (N_BACK, N_BACK), 0)
    kj = lax.broadcasted_iota(I32, (N_BACK, N_BACK), 1)
    ok_own = kj <= qi
    ok_before = kj >= qi
    ok_first = (kj + jnp.where(i > 0, 0, -2 * N_BACK)) >= qi
    dn = (((1,), (1,)), ((), ()))
    subs = [slice(u * N_BACK, (u + 1) * N_BACK) for u in range(nsub)]

    def col(c, w, h):
        start = (c * 3 + w) * BRANCH_W + h * HEAD_DIM
        return slice(start, start + HEAD_DIM)

    units = [(c, h, u) for c in range(ncls) for h in range(N_HEADS) for u in range(nsub)]
    scores = []
    for c, h, u in units:
        q = cur_ref[0, subs[u], col(c, 0, h)]
        k_before = prev_ref[0, :, col(c, 1, h)] if u == 0 else cur_ref[0, subs[u - 1], col(c, 1, h)]
        s_before = lax.dot_general(q, k_before, dn, preferred_element_type=F32)
        s_own = lax.dot_general(q, cur_ref[0, subs[u], col(c, 1, h)], dn, preferred_element_type=F32)
        scores.append((jnp.where(ok_first if u == 0 else ok_before, s_before, NEG),
                       jnp.where(ok_own, s_own, NEG)))
    probs = []
    for s_before, s_own in scores:
        m = jnp.max(jnp.maximum(s_before, s_own), axis=-1, keepdims=True)
        probs.append((jnp.exp2(s_before - m).astype(BF16), jnp.exp2(s_own - m).astype(BF16), m))
    ones = jnp.ones((N_BACK, HEAD_DIM), BF16)
    for (c, h, u), (p_before, p_own, m) in zip(units, probs):
        v_before = prev_ref[0, :, col(c, 2, h)] if u == 0 else cur_ref[0, subs[u - 1], col(c, 2, h)]
        ol = (jnp.dot(p_before, jnp.concatenate([v_before, ones], axis=1), preferred_element_type=F32)
              + jnp.dot(p_own, jnp.concatenate([cur_ref[0, subs[u], col(c, 2, h)], ones], axis=1),
                        preferred_element_type=F32))
        l = ol[:, HEAD_DIM:]
        out = slice(c * BRANCH_W + h * HEAD_DIM, c * BRANCH_W + (h + 1) * HEAD_DIM)
        o_ref[0, subs[u], out] = (ol[:, :HEAD_DIM] / l).astype(o_ref.dtype)
        lse_ref[0, subs[u], out] = m + jnp.log2(l)


def att_c(qkv, dil):
    b, length, _ = qkv.shape
    tq = min(ATTC_ROWS, length)
    ncls = min(dil, ATTC_ROWS // tq)
    r = tq // N_BACK
    groups = dil // ncls
    width = ncls * 3 * BRANCH_W
    out = pl.BlockSpec((1, tq, ncls * BRANCH_W), lambda n, i: (n // groups, i, n % groups))
    return pl.pallas_call(
        _attc_body,
        grid=(b * groups, length // tq),
        in_specs=[pl.BlockSpec((1, tq, width), lambda n, i: (n // groups, i, n % groups)),
                  pl.BlockSpec((1, N_BACK, width), lambda n, i: (n // groups, jnp.maximum(i * r - 1, 0), n % groups))],
        out_specs=[out, out],
        out_shape=[_sds((b, length, dil * BRANCH_W), BF16), _sds((b, length, dil * BRANCH_W), F32)],
        compiler_params=_cparams(("arbitrary", "arbitrary")),
        name="att_c",
    )(qkv, qkv)


def _merge_c_body(dils, *refs):
    n = len(dils)
    o_refs, l_refs, y_ref, scr = refs[:n], refs[n:2 * n], refs[2 * n], refs[2 * n + 1:]
    tm = y_ref.shape[0]

    def natural(ref, dil, buf):
        if dil == 1:
            return ref[0].astype(F32)
        for h in range(N_HEADS):
            for r in range(dil):
                c = r * BRANCH_W + h * HEAD_DIM
                buf[h, pl.ds(r, tm // dil, stride=dil), :] = ref[0, :, c:c + HEAD_DIM].astype(F32)
        return jnp.concatenate([buf[h] for h in range(N_HEADS)], axis=-1)

    outs = [natural(o_refs[g], dils[g], scr[2 * g]) for g in range(n)]
    lses = [natural(l_refs[g], dils[g], scr[2 * g + 1]) for g in range(n)]
    m = functools.reduce(jnp.maximum, lses)
    es = [jnp.exp2(l - m) for l in lses]
    y = functools.reduce(lambda a, b: a + b, [e * o for e, o in zip(es, outs)])
    y_ref[...] = (y / functools.reduce(lambda a, b: a + b, es)).astype(y_ref.dtype)


def merge_c(outs, lses, dils, tm=1024):
    batch, seq = outs[0].shape[0], outs[0].shape[1] * dils[0]
    t = batch * seq
    ns = seq // tm
    specs = [pl.BlockSpec((1, tm // d, d * BRANCH_W), lambda i: (i // ns, i % ns, 0)) for d in dils]
    return pl.pallas_call(
        functools.partial(_merge_c_body, tuple(dils)),
        grid=(t // tm,),
        in_specs=specs + specs,
        out_specs=pl.BlockSpec((tm, BRANCH_W), lambda i: (i, 0)),
        out_shape=_sds((t, BRANCH_W), BF16),
        scratch_shapes=[pltpu.VMEM((N_HEADS, tm, HEAD_DIM), F32) for _ in range(2 * len(dils))],
        compiler_params=_cparams(("arbitrary",)),
        name="merge_c",
    )(*outs, *lses)


def _prep_d_body(zq_ref, zk_ref, zv_ref, gq_ref, gk_ref, cos_ref, sin_ref, q1_ref, q2_ref, k_ref, vt_ref):
    tm = zq_ref.shape[0]
    lane = lax.broadcasted_iota(I32, (tm, LANES), 1)
    lo = lane < DIFF_DIM
    cos = cos_ref[...]
    sin = sin_ref[...]
    zq = zq_ref[...].astype(F32)
    zk = zk_ref[...].astype(F32)
    ones, perm = _group_matrices(DIFF_DIM)
    for h in range(N_HEADS):
        sl = slice(h * HEAD_DIM, (h + 1) * HEAD_DIM)
        q = _norm_rope(zq[:, sl], gq_ref[...], cos, sin, ones, perm, DIFF_DIM)
        q1_ref[:, sl] = jnp.where(lo, q, 0.0).astype(BF16)
        q2_ref[:, sl] = jnp.where(lo, 0.0, q).astype(BF16)
        k_ref[:, sl] = _norm_rope(zk[:, sl], gk_ref[...], cos, sin, ones, perm, DIFF_DIM).astype(BF16)
    vt_ref[0] = zv_ref[...].astype(F32).T.astype(BF16)


def prep_d(z, qn, kn, cos_t, sin_t, seq, tm=DIFF_KTILE):
    t = z.shape[0]
    ns = seq // tm
    c0 = (W_A + W_B + W_C) // BRANCH_W
    gq = jnp.tile(qn * (DIFF_DIM ** -0.5 * math.log2(math.e)), 2).reshape(1, LANES).astype(F32)
    gk = jnp.tile(kn, 2).reshape(1, LANES).astype(F32)
    row = pl.BlockSpec((tm, BRANCH_W), lambda i: (i, 0))
    tab = pl.BlockSpec((tm, LANES), lambda i: (i % ns, 0))
    vec = pl.BlockSpec((1, LANES), lambda i: (0, 0))
    return pl.pallas_call(
        _prep_d_body,
        grid=(t // tm,),
        in_specs=[pl.BlockSpec((tm, BRANCH_W), lambda i: (i, c0)),
                  pl.BlockSpec((tm, BRANCH_W), lambda i: (i, c0 + 1)),
                  pl.BlockSpec((tm, BRANCH_W), lambda i: (i, c0 + 2)), vec, vec, tab, tab],
        out_specs=[row, row, row, pl.BlockSpec((1, BRANCH_W, tm), lambda i: (i, 0, 0))],
        out_shape=[_sds((t, BRANCH_W), BF16)] * 3 + [_sds((t // tm, BRANCH_W, tm), BF16)],
        compiler_params=_cparams(("arbitrary",)),
        name="prep_d",
    )(z, z, z, gq, gk, cos_t, sin_t)


def _diff_body(lam_init, q1_ref, q2_ref, k_ref, vt_ref, lq1, lk1, lq2, lk2, sg_ref, o_ref, acc_ref):
    i = pl.program_id(1)
    tq = q1_ref.shape[1]
    n_chain = 2 * N_HEADS
    lam = (jnp.exp(jnp.sum(lq1[...] * lk1[...], axis=-1, keepdims=True))
           - jnp.exp(jnp.sum(lq2[...] * lk2[...], axis=-1, keepdims=True)) + lam_init)
    tk = vt_ref.shape[2]
    dn = (((1,), (1,)), ((), ()))
    acc_ref[...] = jnp.zeros_like(acc_ref)
    heads = [slice(h * HEAD_DIM, (h + 1) * HEAD_DIM) for h in range(N_HEADS)]

    def step(j, carry, n_keys, shift):
        ms, ls = carry
        start = pl.multiple_of(j * tk, tk)
        scores = []
        for c in range(n_chain):
            q = (q1_ref, q2_ref)[c % 2][0, :, heads[c // 2]]
            scores.append(lax.dot_general(k_ref[0, pl.ds(start, n_keys), heads[c // 2]], q, dn,
                                          preferred_element_type=F32))
        if shift is not None:
            krow = lax.broadcasted_iota(I32, (n_keys, tq), 0)
            qcol = lax.broadcasted_iota(I32, (n_keys, tq), 1)
            visible = krow <= qcol + shift
        new_m, new_l, alphas, probs = [], [], [], []
        for c in range(n_chain):
            s = scores[c] if shift is None else jnp.where(visible, scores[c], NEG)
            m_new = jnp.maximum(ms[c], jnp.max(s, axis=0, keepdims=True))
            a = jnp.exp2(ms[c] - m_new)
            p = jnp.exp2(s - m_new)
            new_l.append(a * ls[c] + jnp.sum(p, axis=0, keepdims=True))
            new_m.append(m_new)
            alphas.append(a)
            probs.append(p.astype(BF16))
        for c in range(n_chain):
            pv = jnp.dot(vt_ref[j, heads[c // 2], 0:n_keys], probs[c], preferred_element_type=F32)
            acc_ref[c] = alphas[c] * acc_ref[c] + pv
        return tuple(new_m), tuple(new_l)

    init = (tuple(jnp.full((1, tq), NEG, F32) for _ in range(n_chain)),
            tuple(jnp.zeros((1, tq), F32) for _ in range(n_chain)))
    ratio = tk // tq
    n_full = i // ratio
    carry = lax.fori_loop(0, n_full, functools.partial(step, n_keys=tk, shift=None), init)
    if ratio == 1:
        _, ls = step(n_full, carry, tk, 0)
    else:
        _, ls = lax.cond(i % 2 == 1,
                         lambda c: step(n_full, c, tk, tq),
                         lambda c: step(n_full, c, tq, 0), carry)
    for h in range(N_HEADS):
        o1 = acc_ref[2 * h] * (1.0 / ls[2 * h])
        o2 = acc_ref[2 * h + 1] * (1.0 / ls[2 * h + 1])
        a = o1 - lam * o2
        ms = jnp.mean(a * a, axis=0, keepdims=True)
        y = a * lax.rsqrt(ms + EPS) * sg_ref[...]
        o_ref[0, :, h * HEAD_DIM:(h + 1) * HEAD_DIM] = y.T.astype(o_ref.dtype)


def diff_att(q1, q2, k, vt, lq1, lk1, lq2, lk2, sub_g, lam_init, tq=DIFF_TILE):
    b, s, _ = q1.shape
    tk = vt.shape[2]
    assert tk in (tq, 2 * tq)
    nk = s // tq
    qs = pl.BlockSpec((1, tq, BRANCH_W), lambda n, i: (n, i, 0))
    vec = pl.BlockSpec((1, DIFF_DIM), lambda n, i: (0, 0))
    sg = jnp.broadcast_to((sub_g * (1.0 - lam_init))[:, None], (HEAD_DIM, tq)).astype(F32)
    return pl.pallas_call(
        functools.partial(_diff_body, lam_init),
        grid=(b, nk),
        in_specs=[qs, qs,
                  pl.BlockSpec((1, s, BRANCH_W), lambda n, i: (n, 0, 0)),
                  pl.BlockSpec((s // tk, BRANCH_W, tk), lambda n, i: (n, 0, 0)),
                  vec, vec, vec, vec,
                  pl.BlockSpec((HEAD_DIM, tq), lambda n, i: (0, 0))],
        out_specs=qs,
        out_shape=_sds((b, s, BRANCH_W), BF16),
        scratch_shapes=[pltpu.VMEM((2 * N_HEADS, HEAD_DIM, tq), F32)],
        compiler_params=_cparams(("arbitrary", "arbitrary")),
        name="diff_att",
    )(q1, q2, k, vt, lq1.reshape(1, -1), lk1.reshape(1, -1), lq2.reshape(1, -1), lk2.reshape(1, -1), sg)


def _gate_body(h_ref, ya, yb, yc, yd, wg_ref, bg_ref, wb_ref, o_ref):
    h = h_ref[...]
    acc = None
    for i, y in enumerate((ya, yb, yc, yd)):
        g = jax.nn.sigmoid(jnp.dot(h, wg_ref[i], preferred_element_type=F32) + bg_ref[i:i + 1, :])
        t = jnp.dot(y[...], wb_ref[i], preferred_element_type=F32)
        acc = g * t if acc is None else acc + g * t
    o_ref[...] = acc.astype(o_ref.dtype)


def gate_merge(h, ys, w_gate, b_gate, w_branch, tm=512, tn=512):
    t, d = h.shape
    ysp = pl.BlockSpec((tm, BRANCH_W), lambda j, i: (i, 0))
    return pl.pallas_call(
        _gate_body,
        grid=(d // tn, t // tm),
        in_specs=[pl.BlockSpec((tm, d), lambda j, i: (i, 0)), ysp, ysp, ysp, ysp,
                  pl.BlockSpec((4, d, tn), lambda j, i: (0, 0, j)),
                  pl.BlockSpec((4, tn), lambda j, i: (0, j)),
                  pl.BlockSpec((4, BRANCH_W, tn), lambda j, i: (0, 0, j))],
        out_specs=pl.BlockSpec((tm, tn), lambda j, i: (i, j)),
        out_shape=_sds((t, d), BF16),
        compiler_params=_cparams(("arbitrary", "arbitrary")),
        name="gate_merge",
    )(h, *ys, w_gate, b_gate, w_branch)


def _store_rows(ref, x):
    m = x.shape[0]
    for c in range(PACK_ROWS):
        ref[pl.ds(c, m, stride=PACK_ROWS), :] = x[:, c * LANES:(c + 1) * LANES]


def _load_rows(ref, m, dtype):
    return jnp.concatenate([ref[pl.ds(c, m, stride=PACK_ROWS), :].astype(dtype) for c in range(PACK_ROWS)], axis=-1)


def _outproj_body(m_ref, x_ref, wo_ref, g_ref, wr_ref, br_ref, x1_ref, hp_ref, meta_ref, cnt_ref, imeta_ref,
                  run_ref):
    @pl.when(pl.program_id(0) == 0)
    def _():
        run_ref[...] = jnp.zeros_like(run_ref)

    sub = x_ref.shape[0] // ROUTE_SPLIT
    x1s = []
    for s in range(ROUTE_SPLIT):
        rows = slice(s * sub, (s + 1) * sub)
        x1 = x_ref[rows, :] + jnp.dot(m_ref[rows, :], wo_ref[...], preferred_element_type=F32)
        x1_ref[rows, :] = x1
        x1s.append(x1)
    run = run_ref[0:1, :]
    for s in range(ROUTE_SPLIT):
        rows = slice(s * sub, (s + 1) * sub)
        ms = jnp.mean(x1s[s] * x1s[s], axis=-1, keepdims=True)
        h2 = x1s[s] * lax.rsqrt(ms + EPS) * g_ref[...]
        _store_rows(hp_ref.at[pl.ds(s * sub * PACK_ROWS, sub * PACK_ROWS), :], h2)
        meta, run = _route(h2, wr_ref, br_ref, run)
        meta_ref[rows, :] = meta
        imeta_ref[0, :, rows] = meta.T[0:8, :].astype(I32)
    run_ref[...] = jnp.broadcast_to(run, run_ref.shape)
    cnt_ref[...] = jnp.broadcast_to(run, cnt_ref.shape)


def _route(h2, wr_ref, br_ref, run):
    tm = h2.shape[0]
    lg = jnp.dot(h2.astype(BF16), wr_ref[...], preferred_element_type=F32) + br_ref[...]
    lane = lax.broadcasted_iota(I32, (tm, LANES), 1)
    big = jnp.int32(4 * LANES)
    is_g = (lane >= N_EXPERTS) & (lane < N_EXPERTS + N_GROUPS)
    gl = jnp.where(is_g, lg, NEG)
    gm = jnp.max(gl, axis=-1, keepdims=True)
    g_p = 1.0 / jnp.sum(jnp.exp(gl - gm), axis=-1, keepdims=True)
    gidx = jnp.min(jnp.where(gl == gm, lane, big), axis=-1, keepdims=True) - N_EXPERTS
    lo_e = gidx * EXP_PER_GROUP
    in_g = (lane >= lo_e) & (lane < lo_e + EXP_PER_GROUP)
    el = jnp.where(in_g, lg, NEG)
    v1 = jnp.max(el, axis=-1, keepdims=True)
    i1 = jnp.min(jnp.where(el == v1, lane, big), axis=-1, keepdims=True)
    el2 = jnp.where(lane == i1, NEG, el)
    v2 = jnp.max(el2, axis=-1, keepdims=True)
    i2 = jnp.min(jnp.where(el2 == v2, lane, big), axis=-1, keepdims=True)
    e2 = jnp.exp(v2 - v1)
    w1 = g_p / (1.0 + e2)
    w2 = g_p * e2 / (1.0 + e2)

    oh1 = lane == i1
    oh2 = lane == i2
    a = jnp.where(oh1, 1.0, jnp.where(oh2, 1.0, 0.0))
    r = lax.broadcasted_iota(I32, (tm, tm), 0)
    c = lax.broadcasted_iota(I32, (tm, tm), 1)
    before = jnp.where(c < r, 1.0, 0.0).astype(BF16)
    pref = jnp.dot(before, a.astype(BF16), preferred_element_type=F32) + run
    rank1 = jnp.sum(jnp.where(oh1, pref, 0.0), axis=-1, keepdims=True)
    rank2 = jnp.sum(jnp.where(oh2, pref, 0.0), axis=-1, keepdims=True)

    meta = jnp.where(lane == 0, i1.astype(F32), 0.0)
    meta = jnp.where(lane == 1, i2.astype(F32), meta)
    meta = jnp.where(lane == 2, w1, meta)
    meta = jnp.where(lane == 3, w2, meta)
    meta = jnp.where(lane == 4, rank1, meta)
    meta = jnp.where(lane == 5, rank2, meta)
    return meta, run + jnp.sum(a, axis=0, keepdims=True)


def outproj_route(merged, x, w_out, g2, w_r, b_r, tm=ROUTE_TILE):
    t, d = x.shape
    return pl.pallas_call(
        _outproj_body,
        grid=(t // tm,),
        in_specs=[pl.BlockSpec((tm, d), lambda i: (i, 0)),
                  pl.BlockSpec((tm, d), lambda i: (i, 0)),
                  pl.BlockSpec((d, d), lambda i: (0, 0)),
                  pl.BlockSpec((1, d), lambda i: (0, 0)),
                  pl.BlockSpec((d, LANES), lambda i: (0, 0)),
                  pl.BlockSpec((1, LANES), lambda i: (0, 0))],
        out_specs=[pl.BlockSpec((tm, d), lambda i: (i, 0)),
                   pl.BlockSpec((tm * PACK_ROWS, LANES), lambda i: (i, 0)),
                   pl.BlockSpec((tm, LANES), lambda i: (i, 0)),
                   pl.BlockSpec((8, LANES), lambda i: (0, 0)),
                   pl.BlockSpec((1, 8, tm), lambda i: (i, 0, 0))],
        out_shape=[_sds((t, d), F32), _sds((t * PACK_ROWS, LANES), F32), _sds((t, LANES), F32),
                   _sds((8, LANES), F32), _sds((t // tm, 8, tm), I32)],
        scratch_shapes=[pltpu.VMEM((8, LANES), F32)],
        compiler_params=_cparams(("arbitrary",)),
        name="outproj_route",
    )(merged, x, w_out, g2.reshape(1, d), w_r, b_r)


def _slot(seg_ref, imeta_ref, k, t):
    return seg_ref[imeta_ref[0, k, t]] + imeta_ref[0, 4 + k, t]


def _zero_fill(plan_ref, xs_ref, zero_ref, zsem, wait):
    def chunk(first_row, n_rows):
        cp = pltpu.make_async_copy(
            zero_ref.at[pl.ds(0, n_rows * PACK_ROWS), :],
            xs_ref.at[pl.ds(pl.multiple_of(first_row * PACK_ROWS, PACK_ROWS), n_rows * PACK_ROWS), :], zsem)
        if wait:
            cp.wait()
        else:
            cp.start()

    def per_expert(e, carry):
        row = plan_ref[N_EXPERTS + e]
        n = plan_ref[2 * N_EXPERTS + e]
        bit = ZERO_ROWS
        while bit >= 1:
            has = (n & bit) != 0
            pl.when(has)(functools.partial(chunk, row, bit))
            row = row + jnp.where(has, bit, 0)
            bit //= 2
        return carry

    def per_tail(j, carry):
        chunk(plan_ref[3 * N_EXPERTS] + j * ZERO_ROWS, ZERO_ROWS)
        return carry

    lax.fori_loop(0, N_EXPERTS, per_expert, 0)
    lax.fori_loop(0, plan_ref[3 * N_EXPERTS + 1], per_tail, 0)


def _dispatch_body(seg_ref, imeta_ref, hp_ref, xs_ref, zero_ref, sem, zsem):
    tm = hp_ref.shape[0] // PACK_ROWS
    first = pl.program_id(0) == 0

    @pl.when(first)
    def _():
        zero_ref[...] = jnp.zeros_like(zero_ref)
        _zero_fill(seg_ref, xs_ref, zero_ref, zsem, wait=False)

    def row_copy(t, k):
        pos = _slot(seg_ref, imeta_ref, k, t)
        src = hp_ref.at[pl.ds(pl.multiple_of(t * PACK_ROWS, PACK_ROWS), PACK_ROWS), :]
        dst = xs_ref.at[pl.ds(pl.multiple_of(pos * PACK_ROWS, PACK_ROWS), PACK_ROWS), :]
        return pltpu.make_async_copy(src, dst, sem)

    def start(t, carry):
        row_copy(t, 0).start()
        row_copy(t, 1).start()
        return carry

    lax.fori_loop(0, tm, start, 0, unroll=DMA_UNROLL)
    whole = pltpu.make_async_copy(hp_ref, xs_ref.at[pl.ds(0, tm * PACK_ROWS), :], sem)
    whole.wait()
    whole.wait()

    @pl.when(first)
    def _():
        _zero_fill(seg_ref, xs_ref, zero_ref, zsem, wait=True)


def dispatch(hp, plan, imeta, rows):
    nt, _, tm = imeta.shape
    grid_spec = pltpu.PrefetchScalarGridSpec(
        num_scalar_prefetch=1,
        grid=(nt,),
        in_specs=[pl.BlockSpec((1, 8, tm), lambda i, seg: (i, 0, 0), memory_space=pltpu.SMEM),
                  pl.BlockSpec((tm * PACK_ROWS, LANES), lambda i, seg: (i, 0))],
        out_specs=pl.BlockSpec(memory_space=pl.ANY),
        scratch_shapes=[pltpu.VMEM((ZERO_ROWS * PACK_ROWS, LANES), F32), pltpu.SemaphoreType.DMA(()),
                        pltpu.SemaphoreType.DMA(())],
    )
    return pl.pallas_call(
        _dispatch_body,
        grid_spec=grid_spec,
        out_shape=_sds((rows * PACK_ROWS, LANES), F32),
        compiler_params=_cparams(("arbitrary",), has_side_effects=True),
        name="dispatch",
    )(plan, imeta, hp)


def _expert_body(te_ref, tv_ref, xs_ref, wu_ref, wd_ref, ys_ref, wu_bf, wd_bf):
    i = pl.program_id(0)
    tr = xs_ref.shape[0] // PACK_ROWS

    @pl.when(jnp.logical_or(i == 0, te_ref[i] != te_ref[jnp.maximum(i - 1, 0)]))
    def _():
        wu_bf[...] = wu_ref[0, 0].astype(BF16)
        wd_bf[...] = wd_ref[0, 0].astype(BF16)

    @pl.when(tv_ref[i] == 0)
    def _():
        ys_ref[...] = jnp.zeros_like(ys_ref)

    @pl.when(tv_ref[i] > 0)
    def _():
        x = _load_rows(xs_ref, tr, BF16)
        ac = jnp.dot(x, wu_bf[...], preferred_element_type=F32)
        a = ac[:, :D_EXPERT]
        c = ac[:, D_EXPERT:]
        hmid = (a * jax.nn.sigmoid(a) * c).astype(BF16)
        y = jnp.dot(hmid, wd_bf[...], preferred_element_type=F32)
        _store_rows(ys_ref, y)


def experts(xs, tile_expert, tile_valid, w_up, w_down, layer_idx, tr=ROW_TILE):
    rows = xs.shape[0] // PACK_ROWS
    nt = rows // tr
    grid_spec = pltpu.PrefetchScalarGridSpec(
        num_scalar_prefetch=2,
        grid=(nt,),
        in_specs=[pl.BlockSpec((tr * PACK_ROWS, LANES), lambda i, te, tv: (i, 0)),
                  pl.BlockSpec((1, 1, D_MODEL, 2 * D_EXPERT), lambda i, te, tv: (layer_idx, te[i], 0, 0)),
                  pl.BlockSpec((1, 1, D_EXPERT, D_MODEL), lambda i, te, tv: (layer_idx, te[i], 0, 0))],
        out_specs=pl.BlockSpec((tr * PACK_ROWS, LANES), lambda i, te, tv: (i, 0)),
        scratch_shapes=[pltpu.VMEM((D_MODEL, 2 * D_EXPERT), BF16), pltpu.VMEM((D_EXPERT, D_MODEL), BF16)],
    )
    return pl.pallas_call(
        _expert_body,
        grid_spec=grid_spec,
        out_shape=_sds((rows * PACK_ROWS, LANES), F32),
        compiler_params=_cparams(("arbitrary",)),
        name="experts",
    )(tile_expert, tile_valid, xs, w_up, w_down)


def _combine_body(seg_ref, imeta_ref, imeta_next_ref, ys_ref, x_ref, meta_ref, o_ref, buf, sem):
    i = pl.program_id(0)
    tm = x_ref.shape[0]
    slot = i % 2

    def gather(im_ref, dst_slot):
        def start(t, carry):
            for k in range(2):
                pos = _slot(seg_ref, im_ref, k, t)
                src = ys_ref.at[pl.ds(pl.multiple_of(pos * PACK_ROWS, PACK_ROWS), PACK_ROWS), :]
                dst = buf.at[dst_slot, k, pl.ds(pl.multiple_of(t * PACK_ROWS, PACK_ROWS), PACK_ROWS), :]
                pltpu.make_async_copy(src, dst, sem.at[dst_slot]).start()
            return carry

        lax.fori_loop(0, tm, start, 0, unroll=DMA_UNROLL)

    @pl.when(i == 0)
    def _():
        gather(imeta_ref, 0)

    @pl.when(i + 1 < pl.num_programs(0))
    def _():
        gather(imeta_next_ref, 1 - slot)

    for k in range(2):
        pltpu.make_async_copy(ys_ref.at[pl.ds(0, tm * PACK_ROWS), :], buf.at[slot, k], sem.at[slot]).wait()
    meta = meta_ref[...]
    w1 = meta[:, 2:3]
    w2 = meta[:, 3:4]
    for c in range(PACK_ROWS):
        cols = slice(c * LANES, (c + 1) * LANES)
        o_ref[:, cols] = (x_ref[:, cols] + w1 * buf[slot, 0, pl.ds(c, tm, stride=PACK_ROWS), :]
                          + w2 * buf[slot, 1, pl.ds(c, tm, stride=PACK_ROWS), :])


def combine(ys, seg_start, imeta, x1, meta):
    t, d = x1.shape
    nt, _, tm = imeta.shape
    grid_spec = pltpu.PrefetchScalarGridSpec(
        num_scalar_prefetch=1,
        grid=(nt,),
        in_specs=[pl.BlockSpec((1, 8, tm), lambda i, seg: (i, 0, 0), memory_space=pltpu.SMEM),
                  pl.BlockSpec((1, 8, tm), lambda i, seg: (jnp.minimum(i + 1, nt - 1), 0, 0),
                               memory_space=pltpu.SMEM),
                  pl.BlockSpec(memory_space=pl.ANY),
                  pl.BlockSpec((tm, d), lambda i, seg: (i, 0)),
                  pl.BlockSpec((tm, LANES), lambda i, seg: (i, 0))],
        out_specs=pl.BlockSpec((tm, d), lambda i, seg: (i, 0)),
        scratch_shapes=[pltpu.VMEM((2, 2, tm * PACK_ROWS, LANES), F32), pltpu.SemaphoreType.DMA((2,))],
    )
    return pl.pallas_call(
        _combine_body,
        grid_spec=grid_spec,
        out_shape=_sds((t, d), F32),
        compiler_params=_cparams(("arbitrary",)),
        name="combine",
    )(seg_start, imeta, imeta, ys, x1, meta)


def moe_plan(counts, n_tokens, tr=ROW_TILE):
    n_tiles = (2 * n_tokens) // tr + N_EXPERTS
    cnt = counts[0, :N_EXPERTS].astype(I32)
    tiles = (cnt + tr - 1) // tr
    e = jnp.arange(N_EXPERTS, dtype=I32)
    tile_end = jnp.sum(jnp.where(e[:, None] <= e[None, :], tiles[:, None], 0), axis=0)
    seg_start = (tile_end - tiles) * tr
    n_used = tile_end[N_EXPERTS - 1]
    tid = jnp.arange(n_tiles, dtype=I32)
    tile_valid = (tid < n_used).astype(I32)
    last = jnp.maximum(n_used - 1, 0)
    tile_expert = jnp.sum((tile_end[None, :] <= jnp.minimum(tid, last)[:, None]).astype(I32), axis=1)
    tile_expert = jnp.minimum(tile_expert, N_EXPERTS - 1)
    tail = jnp.stack([n_used * tr, (n_tiles - n_used) * (tr // ZERO_ROWS)])
    plan = jnp.concatenate([seg_start, seg_start + cnt, tiles * tr - cnt, tail]).astype(I32)
    return plan, tile_expert, tile_valid, n_tiles * tr


def mixer_c(z, qn, kn, cos_t, sin_t, batch):
    outs, lses, dils = [], [], []
    for g, (window, dil) in enumerate(DIL_CFG):
        assert window // dil == N_BACK
        o, l = att_c(prep_c(z, g, dil, qn, kn, cos_t, sin_t, batch), dil)
        outs.append(o)
        lses.append(l)
        dils.append(dil)
    return merge_c(outs, lses, dils)


def layer(x, p, lam_init, batch, tabs):
    t, d = x.shape
    s = t // batch
    z, h = proj(x, p["norm_mix"], p["w_in"])
    y_a = mix_a(z, p["sg_norm"], p["sg_w"], p["sg_b"])
    y_b = mix_b(z, p["conv_w"], batch)
    y_c = mixer_c(z, p["qn_c"], p["kn_c"], tabs[0], tabs[1], batch)
    q1, q2, kd, vt = prep_d(z, p["qn_d"], p["kn_d"], tabs[2], tabs[3], s)
    y_d = diff_att(q1.reshape(batch, s, -1), q2.reshape(batch, s, -1), kd.reshape(batch, s, -1), vt,
                   p["lam_q1"], p["lam_k1"], p["lam_q2"], p["lam_k2"], p["subln_d"],
                   lam_init).reshape(t, BRANCH_W)
    merged = gate_merge(h, (y_a, y_b, y_c, y_d), p["w_gate"], p["b_gate"], p["w_branch"])
    x1, hp, meta, counts, imeta = outproj_route(merged, x, p["w_out"], p["norm_ffn"], p["w_r"], p["b_r"])
    seg_start, tile_expert, tile_valid, rows = moe_plan(counts, t)
    xs = dispatch(hp, seg_start, imeta, rows)
    ys = experts(xs, tile_expert, tile_valid, p["w_up"], p["w_down"], p["layer"])
    return combine(ys, seg_start, imeta, x1, meta)


def kernel(x, norm_mix, w_in, sg_norm, sg_w, sg_b, conv_w, qn_c, kn_c, qn_d, kn_d, lam_q1, lam_k1, lam_q2,
           lam_k2, subln_d, w_gate, b_gate, w_branch, w_out, norm_ffn, w_rg, b_rg, w_re, b_re, w_up, w_down):
    batch, seq, d = x.shape
    depth = w_in.shape[0]
    tabs = rope_tables(seq, HEAD_DIM // 2) + rope_tables(seq, DIFF_DIM // 2)
    xf = x.reshape(batch * seq, d)
    for l in range(depth):
        pad = LANES - N_EXPERTS - N_GROUPS
        w_r = jnp.concatenate([w_re[l], w_rg[l], jnp.zeros((d, pad), F32)], axis=1).astype(BF16)
        b_r = jnp.concatenate([b_re[l], b_rg[l], jnp.zeros((pad,), F32)]).reshape(1, LANES)
        p = dict(norm_mix=norm_mix[l], w_in=w_in[l].astype(BF16), sg_norm=sg_norm[l], sg_w=sg_w[l], sg_b=sg_b[l],
                 conv_w=conv_w[l], qn_c=qn_c[l], kn_c=kn_c[l], qn_d=qn_d[l], kn_d=kn_d[l], lam_q1=lam_q1[l],
                 lam_k1=lam_k1[l], lam_q2=lam_q2[l], lam_k2=lam_k2[l], subln_d=subln_d[l],
                 w_gate=w_gate[l].astype(BF16), b_gate=b_gate[l], w_branch=w_branch[l].astype(BF16),
                 w_out=w_out[l].astype(BF16), norm_ffn=norm_ffn[l], w_r=w_r, b_r=b_r,
                 w_up=w_up, w_down=w_down, layer=l)
        xf = layer(xf, p, 0.8 - 0.6 * math.exp(-0.3 * l), batch, tabs)
    return xf.reshape(batch, seq, d)
```

```python
import functools
import math

import jax
import jax.numpy as jnp
from jax import lax
from jax.experimental import pallas as pl
from jax.experimental.pallas import tpu as pltpu

F32 = jnp.float32
BF16 = jnp.bfloat16
I32 = jnp.int32

D_MODEL = 2048
BRANCH_W = 512
HEAD_DIM = 128
N_HEADS = 4
SG_CHUNK = 128
DIL_CFG = ((128, 1), (512, 4), (2048, 16))
N_BACK = 128
DIFF_DIM = 64
ROPE_THETA = 10000.0
W_A = 2 * BRANCH_W
W_B = 3 * BRANCH_W
W_C = 9 * BRANCH_W
N_GROUPS = 4
EXP_PER_GROUP = 8
N_EXPERTS = 32
D_EXPERT = 512
EPS = 1e-6
NEG = -1e30

LANES = 128
PACK_ROWS = D_MODEL // LANES
VMEM_LIMIT = 56 * 1024 * 1024
ROW_TILE = 512
DIFF_TILE = 512
DIFF_KTILE = 512
ROUTE_TILE = 512
ATTC_ROWS = 512
ZERO_ROWS = 256
ROUTE_SPLIT = 2
DMA_UNROLL = 8


def _cparams(sem, **kw):
    return pltpu.CompilerParams(dimension_semantics=sem, vmem_limit_bytes=VMEM_LIMIT, **kw)


def _sds(shape, dtype):
    return jax.ShapeDtypeStruct(shape, dtype)


def _proj_body(x_ref, g_ref, w_ref, z_ref, h_ref):
    @pl.when(pl.program_id(1) == 0)
    def _():
        x = x_ref[...]
        ms = jnp.mean(x * x, axis=-1, keepdims=True)
        h_ref[...] = (x * lax.rsqrt(ms + EPS) * g_ref[...]).astype(BF16)

    z_ref[...] = jnp.dot(h_ref[...], w_ref[...], preferred_element_type=F32).astype(z_ref.dtype)


def proj(x, g, w, tm=1024, tn=512):
    t, d = x.shape
    n = w.shape[1]
    return pl.pallas_call(
        _proj_body,
        grid=(t // tm, n // tn),
        in_specs=[pl.BlockSpec((tm, d), lambda i, j: (i, 0)),
                  pl.BlockSpec((1, d), lambda i, j: (0, 0)),
                  pl.BlockSpec((d, tn), lambda i, j: (0, j))],
        out_specs=[pl.BlockSpec((tm, tn), lambda i, j: (i, j)),
                   pl.BlockSpec((tm, d), lambda i, j: (i, 0))],
        out_shape=[_sds((t, n), BF16), _sds((t, d), BF16)],
        compiler_params=_cparams(("arbitrary", "arbitrary")),
        name="proj",
    )(x, g.reshape(1, d), w)


def _gelu_tanh(x):
    c = math.sqrt(2.0 / math.pi)
    return 0.5 * x * (1.0 + jnp.tanh(c * (x + 0.044715 * (x * x * x))))


def _mixa_body(z_ref, g_ref, w_ref, b_ref, o_ref):
    tm = z_ref.shape[0]
    ga = _gelu_tanh(z_ref[...].astype(F32))
    u = ga[:, :BRANCH_W]
    v = ga[:, BRANCH_W:]
    ms = jnp.mean(v * v, axis=-1, keepdims=True)
    vn = (v * lax.rsqrt(ms + EPS) * g_ref[...]).astype(BF16)
    row = lax.broadcasted_iota(I32, (SG_CHUNK, SG_CHUNK), 0)
    col = lax.broadcasted_iota(I32, (SG_CHUNK, SG_CHUNK), 1)
    for g in range(N_HEADS):
        wg = jnp.where(row >= col, w_ref[g], 0.0).astype(BF16)
        bias = b_ref[g]
        cs = slice(g * SG_CHUNK, (g + 1) * SG_CHUNK)
        for c in range(tm // SG_CHUNK):
            rs = slice(c * SG_CHUNK, (c + 1) * SG_CHUNK)
            mixed = jnp.dot(wg, vn[rs, cs], preferred_element_type=F32) + bias
            o_ref[rs, cs] = (u[rs, cs] * mixed).astype(o_ref.dtype)


def mix_a(z, sg_norm, sg_w, sg_b, tm=1024):
    t = z.shape[0]
    bias = jnp.broadcast_to(sg_b[:, :, None], (N_HEADS, SG_CHUNK, SG_CHUNK)).astype(F32)
    return pl.pallas_call(
        _mixa_body,
        grid=(t // tm,),
        in_specs=[pl.BlockSpec((tm, W_A), lambda i: (i, 0)),
                  pl.BlockSpec((1, BRANCH_W), lambda i: (0, 0)),
                  pl.BlockSpec((N_HEADS, SG_CHUNK, SG_CHUNK), lambda i: (0, 0, 0)),
                  pl.BlockSpec((N_HEADS, SG_CHUNK, SG_CHUNK), lambda i: (0, 0, 0))],
        out_specs=pl.BlockSpec((tm, BRANCH_W), lambda i: (i, 0)),
        out_shape=_sds((t, BRANCH_W), BF16),
        compiler_params=_cparams(("arbitrary",)),
        name="mix_a",
    )(z, sg_norm.reshape(1, BRANCH_W), sg_w, bias)


def _mixb_body(bg_ref, cg_ref, hb_ref, w_ref, o_ref, prev_ref):
    @pl.when(pl.program_id(1) == 0)
    def _():
        prev_ref[...] = jnp.zeros_like(prev_ref)

    zz = cg_ref[...].astype(F32) * hb_ref[...].astype(F32)
    tm = zz.shape[0]
    row = lax.broadcasted_iota(I32, zz.shape, 0)
    p = prev_ref[...]
    z1 = jnp.where(row == 0, p[7:8, :], pltpu.roll(zz, 1, 0))
    z2 = jnp.where(row == 0, p[6:7, :], jnp.where(row == 1, p[7:8, :], pltpu.roll(zz, 2, 0)))
    w = w_ref[...]
    y = w[0:1, :] * z2 + w[1:2, :] * z1 + w[2:3, :] * zz
    o_ref[...] = (bg_ref[...].astype(F32) * y).astype(o_ref.dtype)
    prev_ref[...] = zz[tm - 8:, :]


def mix_b(z, conv_w, batch, tm=1024):
    t = z.shape[0]
    nt = t // batch // tm
    c0 = W_A // BRANCH_W

    def zspec(k):
        return pl.BlockSpec((tm, BRANCH_W), lambda b, i: (b * nt + i, c0 + k))

    return pl.pallas_call(
        _mixb_body,
        grid=(batch, nt),
        in_specs=[zspec(0), zspec(1), zspec(2), pl.BlockSpec((3, BRANCH_W), lambda b, i: (0, 0))],
        out_specs=pl.BlockSpec((tm, BRANCH_W), lambda b, i: (b * nt + i, 0)),
        out_shape=_sds((t, BRANCH_W), BF16),
        scratch_shapes=[pltpu.VMEM((8, BRANCH_W), F32)],
        compiler_params=_cparams(("arbitrary", "arbitrary")),
        name="mix_b",
    )(z, z, z, conv_w)


def rope_tables(s, half):
    inv = ROPE_THETA ** (-jnp.arange(half, dtype=F32) / half)
    ang = jnp.arange(s, dtype=F32)[:, None] * inv[None, :]
    cos, sin = jnp.cos(ang), jnp.sin(ang)
    reps = LANES // (2 * half)
    cos_t = jnp.tile(jnp.concatenate([cos, cos], axis=-1), (1, reps))
    sin_t = jnp.tile(jnp.concatenate([-sin, sin], axis=-1), (1, reps))
    return cos_t, sin_t


def _group_matrices(width):
    r = lax.broadcasted_iota(I32, (LANES, LANES), 0)
    c = lax.broadcasted_iota(I32, (LANES, LANES), 1)
    partner = (c // width) * width + (c % width + width // 2) % width
    ones = jnp.where((r // width) == (c // width), 1.0, 0.0).astype(BF16)
    perm = jnp.where(r == partner, 1.0, 0.0).astype(BF16)
    return ones, perm


def _norm_rope(xh, gain, cos, sin, ones, perm, width):
    ssq = jnp.dot((xh * xh).astype(BF16), ones, preferred_element_type=F32)
    xn = xh * lax.rsqrt(ssq * (1.0 / width) + EPS) * gain
    rot = jnp.dot(xn.astype(BF16), perm, preferred_element_type=F32)
    return xn * cos + rot * sin


def _prep_c_body(dil, zq_ref, zk_ref, zv_ref, gq_ref, gk_ref, cos_ref, sin_ref, o_ref, scr_ref):
    tm = zq_ref.shape[0]
    cos = cos_ref[...]
    sin = sin_ref[...]

    ones, perm = _group_matrices(HEAD_DIM)

    def norm_rope(xh, gain):
        return _norm_rope(xh, gain, cos, sin, ones, perm, HEAD_DIM)

    width = 3 * BRANCH_W
    for w, ref in enumerate((zq_ref, zk_ref, zv_ref)):
        for h in range(N_HEADS):
            xh = ref[:, h * HEAD_DIM:(h + 1) * HEAD_DIM].astype(F32)
            val = xh if w == 2 else norm_rope(xh, (gq_ref, gk_ref)[w][...])
            c = w * BRANCH_W + h * HEAD_DIM
            if dil == 1:
                o_ref[0, :, c:c + HEAD_DIM] = val.astype(o_ref.dtype)
            else:
                buf = scr_ref.at[w * N_HEADS + h]
                buf[...] = val
                for r in range(dil):
                    o_ref[0, :, r * width + c:r * width + c + HEAD_DIM] = (
                        buf[pl.ds(r, tm // dil, stride=dil), :].astype(o_ref.dtype))


def prep_c(z, g, dil, qn, kn, cos_t, sin_t, batch, tm=1024):
    t = z.shape[0]
    seq = t // batch
    ns = seq // tm
    c0 = (W_A + W_B) // BRANCH_W + g
    gq = (qn * (HEAD_DIM ** -0.5 * math.log2(math.e))).reshape(1, HEAD_DIM).astype(F32)
    gk = kn.reshape(1, HEAD_DIM).astype(F32)
    vec = pl.BlockSpec((1, HEAD_DIM), lambda i: (0, 0))
    tab = pl.BlockSpec((tm, LANES), lambda i: (i % ns, 0))
    return pl.pallas_call(
        functools.partial(_prep_c_body, dil),
        grid=(t // tm,),
        in_specs=[pl.BlockSpec((tm, BRANCH_W), lambda i: (i, c0)),
                  pl.BlockSpec((tm, BRANCH_W), lambda i: (i, c0 + 3)),
                  pl.BlockSpec((tm, BRANCH_W), lambda i: (i, c0 + 6)), vec, vec, tab, tab],
        out_specs=pl.BlockSpec((1, tm // dil, dil * 3 * BRANCH_W), lambda i: (i // ns, i % ns, 0)),
        out_shape=_sds((batch, seq // dil, dil * 3 * BRANCH_W), BF16),
        scratch_shapes=[pltpu.VMEM((3 * N_HEADS, tm, HEAD_DIM), F32)],
        compiler_params=_cparams(("arbitrary",)),
        name="prep_c",
    )(z, z, z, gq, gk, cos_t, sin_t)


def _attc_body(cur_ref, prev_ref, o_ref, lse_ref):
    i = pl.program_id(1)
    nsub = cur_ref.shape[1] // N_BACK
    ncls = cur_ref.shape[2] // (3 * BRANCH_W)
    qi = lax.broadcasted_iota(I32, (N_BACK, N_BACK), 0)
    kj = lax.broadcasted_iota(I32, (N_BACK, N_BACK), 1)
    ok_own = kj <= qi
    ok_before = kj >= qi
    ok_first = (kj + jnp.where(i > 0, 0, -2 * N_BACK)) >= qi
    dn = (((1,), (1,)), ((), ()))
    subs = [slice(u * N_BACK, (u + 1) * N_BACK) for u in range(nsub)]

    def col(c, w, h):
        start = (c * 3 + w) * BRANCH_W + h * HEAD_DIM
        return slice(start, start + HEAD_DIM)

    units = [(c, h, u) for c in range(ncls) for h in range(N_HEADS) for u in range(nsub)]
    scores = []
    for c, h, u in units:
        q = cur_ref[0, subs[u], col(c, 0, h)]
        k_before = prev_ref[0, :, col(c, 1, h)] if u == 0 else cur_ref[0, subs[u - 1], col(c, 1, h)]
        s_before = lax.dot_general(q, k_before, dn, preferred_element_type=F32)
        s_own = lax.dot_general(q, cur_ref[0, subs[u], col(c, 1, h)], dn, preferred_element_type=F32)
        scores.append((jnp.where(ok_first if u == 0 else ok_before, s_before, NEG),
                       jnp.where(ok_own, s_own, NEG)))
    probs = []
    for s_before, s_own in scores:
        m = jnp.max(jnp.maximum(s_before, s_own), axis=-1, keepdims=True)
        probs.append((jnp.exp2(s_before - m).astype(BF16), jnp.exp2(s_own - m).astype(BF16), m))
    ones = jnp.ones((N_BACK, HEAD_DIM), BF16)
    for (c, h, u), (p_before, p_own, m) in zip(units, probs):
        v_before = prev_ref[0, :, col(c, 2, h)] if u == 0 else cur_ref[0, subs[u - 1], col(c, 2, h)]
        ol = (jnp.dot(p_before, jnp.concatenate([v_before, ones], axis=1), preferred_element_type=F32)
              + jnp.dot(p_own, jnp.concatenate([cur_ref[0, subs[u], col(c, 2, h)], ones], axis=1),
                        preferred_element_type=F32))
        l = ol[:, HEAD_DIM:]
        out = slice(c * BRANCH_W + h * HEAD_DIM, c * BRANCH_W + (h + 1) * HEAD_DIM)
        o_ref[0, subs[u], out] = (ol[:, :HEAD_DIM] / l).astype(o_ref.dtype)
        lse_ref[0, subs[u], out] = m + jnp.log2(l)


def att_c(qkv, dil):
    b, length, _ = qkv.shape
    tq = min(ATTC_ROWS, length)
    ncls = min(dil, ATTC_ROWS // tq)
    r = tq // N_BACK
    groups = dil // ncls
    width = ncls * 3 * BRANCH_W
    out = pl.BlockSpec((1, tq, ncls * BRANCH_W), lambda n, i: (n // groups, i, n % groups))
    return pl.pallas_call(
        _attc_body,
        grid=(b * groups, length // tq),
        in_specs=[pl.BlockSpec((1, tq, width), lambda n, i: (n // groups, i, n % groups)),
                  pl.BlockSpec((1, N_BACK, width), lambda n, i: (n // groups, jnp.maximum(i * r - 1, 0), n % groups))],
        out_specs=[out, out],
        out_shape=[_sds((b, length, dil * BRANCH_W), BF16), _sds((b, length, dil * BRANCH_W), F32)],
        compiler_params=_cparams(("arbitrary", "arbitrary")),
        name="att_c",
    )(qkv, qkv)


def _merge_c_body(dils, *refs):
    n = len(dils)
    o_refs, l_refs, y_ref, scr = refs[:n], refs[n:2 * n], refs[2 * n], refs[2 * n + 1:]
    tm = y_ref.shape[0]

    def natural(ref, dil, buf):
        if dil == 1:
            return ref[0].astype(F32)
        for h in range(N_HEADS):
            for r in range(dil):
                c = r * BRANCH_W + h * HEAD_DIM
                buf[h, pl.ds(r, tm // dil, stride=dil), :] = ref[0, :, c:c + HEAD_DIM].astype(F32)
        return jnp.concatenate([buf[h] for h in range(N_HEADS)], axis=-1)

    outs = [natural(o_refs[g], dils[g], scr[2 * g]) for g in range(n)]
    lses = [natural(l_refs[g], dils[g], scr[2 * g + 1]) for g in range(n)]
    m = functools.reduce(jnp.maximum, lses)
    es = [jnp.exp2(l - m) for l in lses]
    y = functools.reduce(lambda a, b: a + b, [e * o for e, o in zip(es, outs)])
    y_ref[...] = (y / functools.reduce(lambda a, b: a + b, es)).astype(y_ref.dtype)


def merge_c(outs, lses, dils, tm=1024):
    batch, seq = outs[0].shape[0], outs[0].shape[1] * dils[0]
    t = batch * seq
    ns = seq // tm
    specs = [pl.BlockSpec((1, tm // d, d * BRANCH_W), lambda i: (i // ns, i % ns, 0)) for d in dils]
    return pl.pallas_call(
        functools.partial(_merge_c_body, tuple(dils)),
        grid=(t // tm,),
        in_specs=specs + specs,
        out_specs=pl.BlockSpec((tm, BRANCH_W), lambda i: (i, 0)),
        out_shape=_sds((t, BRANCH_W), BF16),
        scratch_shapes=[pltpu.VMEM((N_HEADS, tm, HEAD_DIM), F32) for _ in range(2 * len(dils))],
        compiler_params=_cparams(("arbitrary",)),
        name="merge_c",
    )(*outs, *lses)


def _prep_d_body(zq_ref, zk_ref, zv_ref, gq_ref, gk_ref, cos_ref, sin_ref, q1_ref, q2_ref, k_ref, vt_ref):
    tm = zq_ref.shape[0]
    lane = lax.broadcasted_iota(I32, (tm, LANES), 1)
    lo = lane < DIFF_DIM
    cos = cos_ref[...]
    sin = sin_ref[...]
    zq = zq_ref[...].astype(F32)
    zk = zk_ref[...].astype(F32)
    ones, perm = _group_matrices(DIFF_DIM)
    for h in range(N_HEADS):
        sl = slice(h * HEAD_DIM, (h + 1) * HEAD_DIM)
        q = _norm_rope(zq[:, sl], gq_ref[...], cos, sin, ones, perm, DIFF_DIM)
        q1_ref[:, sl] = jnp.where(lo, q, 0.0).astype(BF16)
        q2_ref[:, sl] = jnp.where(lo, 0.0, q).astype(BF16)
        k_ref[:, sl] = _norm_rope(zk[:, sl], gk_ref[...], cos, sin, ones, perm, DIFF_DIM).astype(BF16)
    vt_ref[0] = zv_ref[...].astype(F32).T.astype(BF16)


def prep_d(z, qn, kn, cos_t, sin_t, seq, tm=DIFF_KTILE):
    t = z.shape[0]
    ns = seq // tm
    c0 = (W_A + W_B + W_C) // BRANCH_W
    gq = jnp.tile(qn * (DIFF_DIM ** -0.5 * math.log2(math.e)), 2).reshape(1, LANES).astype(F32)
    gk = jnp.tile(kn, 2).reshape(1, LANES).astype(F32)
    row = pl.BlockSpec((tm, BRANCH_W), lambda i: (i, 0))
    tab = pl.BlockSpec((tm, LANES), lambda i: (i % ns, 0))
    vec = pl.BlockSpec((1, LANES), lambda i: (0, 0))
    return pl.pallas_call(
        _prep_d_body,
        grid=(t // tm,),
        in_specs=[pl.BlockSpec((tm, BRANCH_W), lambda i: (i, c0)),
                  pl.BlockSpec((tm, BRANCH_W), lambda i: (i, c0 + 1)),
                  pl.BlockSpec((tm, BRANCH_W), lambda i: (i, c0 + 2)), vec, vec, tab, tab],
        out_specs=[row, row, row, pl.BlockSpec((1, BRANCH_W, tm), lambda i: (i, 0, 0))],
        out_shape=[_sds((t, BRANCH_W), BF16)] * 3 + [_sds((t // tm, BRANCH_W, tm), BF16)],
        compiler_params=_cparams(("arbitrary",)),
        name="prep_d",
    )(z, z, z, gq, gk, cos_t, sin_t)


def _diff_body(lam_init, q1_ref, q2_ref, k_ref, vt_ref, lq1, lk1, lq2, lk2, sg_ref, o_ref, acc_ref):
    i = pl.program_id(1)
    tq = q1_ref.shape[1]
    n_chain = 2 * N_HEADS
    lam = (jnp.exp(jnp.sum(lq1[...] * lk1[...], axis=-1, keepdims=True))
           - jnp.exp(jnp.sum(lq2[...] * lk2[...], axis=-1, keepdims=True)) + lam_init)
    tk = vt_ref.shape[2]
    dn = (((1,), (1,)), ((), ()))
    acc_ref[...] = jnp.zeros_like(acc_ref)
    heads = [slice(h * HEAD_DIM, (h + 1) * HEAD_DIM) for h in range(N_HEADS)]

    def step(j, carry, n_keys, shift):
        ms, ls = carry
        start = pl.multiple_of(j * tk, tk)
        scores = []
        for c in range(n_chain):
            q = (q1_ref, q2_ref)[c % 2][0, :, heads[c // 2]]
            scores.append(lax.dot_general(k_ref[0, pl.ds(start, n_keys), heads[c // 2]], q, dn,
                                          preferred_element_type=F32))
        if shift is not None:
            krow = lax.broadcasted_iota(I32, (n_keys, tq), 0)
            qcol = lax.broadcasted_iota(I32, (n_keys, tq), 1)
            visible = krow <= qcol + shift
        new_m, new_l, alphas, probs = [], [], [], []
        for c in range(n_chain):
            s = scores[c] if shift is None else jnp.where(visible, scores[c], NEG)
            m_new = jnp.maximum(ms[c], jnp.max(s, axis=0, keepdims=True))
            a = jnp.exp2(ms[c] - m_new)
            p = jnp.exp2(s - m_new)
            new_l.append(a * ls[c] + jnp.sum(p, axis=0, keepdims=True))
            new_m.append(m_new)
            alphas.append(a)
            probs.append(p.astype(BF16))
        for c in range(n_chain):
            pv = jnp.dot(vt_ref[j, heads[c // 2], 0:n_keys], probs[c], preferred_element_type=F32)
            acc_ref[c] = alphas[c] * acc_ref[c] + pv
        return tuple(new_m), tuple(new_l)

    init = (tuple(jnp.full((1, tq), NEG, F32) for _ in range(n_chain)),
            tuple(jnp.zeros((1, tq), F32) for _ in range(n_chain)))
    ratio = tk // tq
    n_full = i // ratio
    carry = lax.fori_loop(0, n_full, functools.partial(step, n_keys=tk, shift=None), init)
    if ratio == 1:
        _, ls = step(n_full, carry, tk, 0)
    else:
        _, ls = lax.cond(i % 2 == 1,
                         lambda c: step(n_full, c, tk, tq),
                         lambda c: step(n_full, c, tq, 0), carry)
    for h in range(N_HEADS):
        o1 = acc_ref[2 * h] * (1.0 / ls[2 * h])
        o2 = acc_ref[2 * h + 1] * (1.0 / ls[2 * h + 1])
        a = o1 - lam * o2
        ms = jnp.mean(a * a, axis=0, keepdims=True)
        y = a * lax.rsqrt(ms + EPS) * sg_ref[...]
        o_ref[0, :, h * HEAD_DIM:(h + 1) * HEAD_DIM] = y.T.astype(o_ref.dtype)


def diff_att(q1, q2, k, vt, lq1, lk1, lq2, lk2, sub_g, lam_init, tq=DIFF_TILE):
    b, s, _ = q1.shape
    tk = vt.shape[2]
    assert tk in (tq, 2 * tq)
    nk = s // tq
    qs = pl.BlockSpec((1, tq, BRANCH_W), lambda n, i: (n, i, 0))
    vec = pl.BlockSpec((1, DIFF_DIM), lambda n, i: (0, 0))
    sg = jnp.broadcast_to((sub_g * (1.0 - lam_init))[:, None], (HEAD_DIM, tq)).astype(F32)
    return pl.pallas_call(
        functools.partial(_diff_body, lam_init),
        grid=(b, nk),
        in_specs=[qs, qs,
                  pl.BlockSpec((1, s, BRANCH_W), lambda n, i: (n, 0, 0)),
                  pl.BlockSpec((s // tk, BRANCH_W, tk), lambda n, i: (n, 0, 0)),
                  vec, vec, vec, vec,
                  pl.BlockSpec((HEAD_DIM, tq), lambda n, i: (0, 0))],
        out_specs=qs,
        out_shape=_sds((b, s, BRANCH_W), BF16),
        scratch_shapes=[pltpu.VMEM((2 * N_HEADS, HEAD_DIM, tq), F32)],
        compiler_params=_cparams(("arbitrary", "arbitrary")),
        name="diff_att",
    )(q1, q2, k, vt, lq1.reshape(1, -1), lk1.reshape(1, -1), lq2.reshape(1, -1), lk2.reshape(1, -1), sg)


def _gate_body(h_ref, ya, yb, yc, yd, wg_ref, bg_ref, wb_ref, o_ref):
    h = h_ref[...]
    acc = None
    for i, y in enumerate((ya, yb, yc, yd)):
        g = jax.nn.sigmoid(jnp.dot(h, wg_ref[i], preferred_element_type=F32) + bg_ref[i:i + 1, :])
        t = jnp.dot(y[...], wb_ref[i], preferred_element_type=F32)
        acc = g * t if acc is None else acc + g * t
    o_ref[...] = acc.astype(o_ref.dtype)


def gate_merge(h, ys, w_gate, b_gate, w_branch, tm=512, tn=512):
    t, d = h.shape
    ysp = pl.BlockSpec((tm, BRANCH_W), lambda j, i: (i, 0))
    return pl.pallas_call(
        _gate_body,
        grid=(d // tn, t // tm),
        in_specs=[pl.BlockSpec((tm, d), lambda j, i: (i, 0)), ysp, ysp, ysp, ysp,
                  pl.BlockSpec((4, d, tn), lambda j, i: (0, 0, j)),
                  pl.BlockSpec((4, tn), lambda j, i: (0, j)),
                  pl.BlockSpec((4, BRANCH_W, tn), lambda j, i: (0, 0, j))],
        out_specs=pl.BlockSpec((tm, tn), lambda j, i: (i, j)),
        out_shape=_sds((t, d), BF16),
        compiler_params=_cparams(("arbitrary", "arbitrary")),
        name="gate_merge",
    )(h, *ys, w_gate, b_gate, w_branch)


def _store_rows(ref, x):
    m = x.shape[0]
    for c in range(PACK_ROWS):
        ref[pl.ds(c, m, stride=PACK_ROWS), :] = x[:, c * LANES:(c + 1) * LANES]


def _load_rows(ref, m, dtype):
    return jnp.concatenate([ref[pl.ds(c, m, stride=PACK_ROWS), :].astype(dtype) for c in range(PACK_ROWS)], axis=-1)


def _outproj_body(m_ref, x_ref, wo_ref, g_ref, wr_ref, br_ref, x1_ref, hp_ref, meta_ref, cnt_ref, imeta_ref,
                  run_ref):
    @pl.when(pl.program_id(0) == 0)
    def _():
        run_ref[...] = jnp.zeros_like(run_ref)

    sub = x_ref.shape[0] // ROUTE_SPLIT
    x1s = []
    for s in range(ROUTE_SPLIT):
        rows = slice(s * sub, (s + 1) * sub)
        x1 = x_ref[rows, :] + jnp.dot(m_ref[rows, :], wo_ref[...], preferred_element_type=F32)
        x1_ref[rows, :] = x1
        x1s.append(x1)
    run = run_ref[0:1, :]
    for s in range(ROUTE_SPLIT):
        rows = slice(s * sub, (s + 1) * sub)
        ms = jnp.mean(x1s[s] * x1s[s], axis=-1, keepdims=True)
        h2 = x1s[s] * lax.rsqrt(ms + EPS) * g_ref[...]
        _store_rows(hp_ref.at[pl.ds(s * sub * PACK_ROWS, sub * PACK_ROWS), :], h2)
        meta, run = _route(h2, wr_ref, br_ref, run)
        meta_ref[rows, :] = meta
        imeta_ref[0, :, rows] = meta.T[0:8, :].astype(I32)
    run_ref[...] = jnp.broadcast_to(run, run_ref.shape)
    cnt_ref[...] = jnp.broadcast_to(run, cnt_ref.shape)


def _route(h2, wr_ref, br_ref, run):
    tm = h2.shape[0]
    lg = jnp.dot(h2.astype(BF16), wr_ref[...], preferred_element_type=F32) + br_ref[...]
    lane = lax.broadcasted_iota(I32, (tm, LANES), 1)
    big = jnp.int32(4 * LANES)
    is_g = (lane >= N_EXPERTS) & (lane < N_EXPERTS + N_GROUPS)
    gl = jnp.where(is_g, lg, NEG)
    gm = jnp.max(gl, axis=-1, keepdims=True)
    g_p = 1.0 / jnp.sum(jnp.exp(gl - gm), axis=-1, keepdims=True)
    gidx = jnp.min(jnp.where(gl == gm, lane, big), axis=-1, keepdims=True) - N_EXPERTS
    lo_e = gidx * EXP_PER_GROUP
    in_g = (lane >= lo_e) & (lane < lo_e + EXP_PER_GROUP)
    el = jnp.where(in_g, lg, NEG)
    v1 = jnp.max(el, axis=-1, keepdims=True)
    i1 = jnp.min(jnp.where(el == v1, lane, big), axis=-1, keepdims=True)
    el2 = jnp.where(lane == i1, NEG, el)
    v2 = jnp.max(el2, axis=-1, keepdims=True)
    i2 = jnp.min(jnp.where(el2 == v2, lane, big), axis=-1, keepdims=True)
    e2 = jnp.exp(v2 - v1)
    w1 = g_p / (1.0 + e2)
    w2 = g_p * e2 / (1.0 + e2)

    oh1 = lane == i1
    oh2 = lane == i2
    a = jnp.where(oh1, 1.0, jnp.where(oh2, 1.0, 0.0))
    r = lax.broadcasted_iota(I32, (tm, tm), 0)
    c = lax.broadcasted_iota(I32, (tm, tm), 1)
    before = jnp.where(c < r, 1.0, 0.0).astype(BF16)
    pref = jnp.dot(before, a.astype(BF16), preferred_element_type=F32) + run
    rank1 = jnp.sum(jnp.where(oh1, pref, 0.0), axis=-1, keepdims=True)
    rank2 = jnp.sum(jnp.where(oh2, pref, 0.0), axis=-1, keepdims=True)

    meta = jnp.where(lane == 0, i1.astype(F32), 0.0)
    meta = jnp.where(lane == 1, i2.astype(F32), meta)
    meta = jnp.where(lane == 2, w1, meta)
    meta = jnp.where(lane == 3, w2, meta)
    meta = jnp.where(lane == 4, rank1, meta)
    meta = jnp.where(lane == 5, rank2, meta)
    return meta, run + jnp.sum(a, axis=0, keepdims=True)


def outproj_route(merged, x, w_out, g2, w_r, b_r, tm=ROUTE_TILE):
    t, d = x.shape
    return pl.pallas_call(
        _outproj_body,
        grid=(t // tm,),
        in_specs=[pl.BlockSpec((tm, d), lambda i: (i, 0)),
                  pl.BlockSpec((tm, d), lambda i: (i, 0)),
                  pl.BlockSpec((d, d), lambda i: (0, 0)),
                  pl.BlockSpec((1, d), lambda i: (0, 0)),
                  pl.BlockSpec((d, LANES), lambda i: (0, 0)),
                  pl.BlockSpec((1, LANES), lambda i: (0, 0))],
        out_specs=[pl.BlockSpec((tm, d), lambda i: (i, 0)),
                   pl.BlockSpec((tm * PACK_ROWS, LANES), lambda i: (i, 0)),
                   pl.BlockSpec((tm, LANES), lambda i: (i, 0)),
                   pl.BlockSpec((8, LANES), lambda i: (0, 0)),
                   pl.BlockSpec((1, 8, tm), lambda i: (i, 0, 0))],
        out_shape=[_sds((t, d), F32), _sds((t * PACK_ROWS, LANES), F32), _sds((t, LANES), F32),
                   _sds((8, LANES), F32), _sds((t // tm, 8, tm), I32)],
        scratch_shapes=[pltpu.VMEM((8, LANES), F32)],
        compiler_params=_cparams(("arbitrary",)),
        name="outproj_route",
    )(merged, x, w_out, g2.reshape(1, d), w_r, b_r)


def _slot(seg_ref, imeta_ref, k, t):
    return seg_ref[imeta_ref[0, k, t]] + imeta_ref[0, 4 + k, t]


def _zero_fill(plan_ref, xs_ref, zero_ref, zsem, wait):
    def chunk(first_row, n_rows):
        cp = pltpu.make_async_copy(
            zero_ref.at[pl.ds(0, n_rows * PACK_ROWS), :],
            xs_ref.at[pl.ds(pl.multiple_of(first_row * PACK_ROWS, PACK_ROWS), n_rows * PACK_ROWS), :], zsem)
        if wait:
            cp.wait()
        else:
            cp.start()

    def per_expert(e, carry):
        row = plan_ref[N_EXPERTS + e]
        n = plan_ref[2 * N_EXPERTS + e]
        bit = ZERO_ROWS
        while bit >= 1:
            has = (n & bit) != 0
            pl.when(has)(functools.partial(chunk, row, bit))
            row = row + jnp.where(has, bit, 0)
            bit //= 2
        return carry

    def per_tail(j, carry):
        chunk(plan_ref[3 * N_EXPERTS] + j * ZERO_ROWS, ZERO_ROWS)
        return carry

    lax.fori_loop(0, N_EXPERTS, per_expert, 0)
    lax.fori_loop(0, plan_ref[3 * N_EXPERTS + 1], per_tail, 0)


def _dispatch_body(seg_ref, imeta_ref, hp_ref, xs_ref, zero_ref, sem, zsem):
    tm = hp_ref.shape[0] // PACK_ROWS
    first = pl.program_id(0) == 0

    @pl.when(first)
    def _():
        zero_ref[...] = jnp.zeros_like(zero_ref)
        _zero_fill(seg_ref, xs_ref, zero_ref, zsem, wait=False)

    def row_copy(t, k):
        pos = _slot(seg_ref, imeta_ref, k, t)
        src = hp_ref.at[pl.ds(pl.multiple_of(t * PACK_ROWS, PACK_ROWS), PACK_ROWS), :]
        dst = xs_ref.at[pl.ds(pl.multiple_of(pos * PACK_ROWS, PACK_ROWS), PACK_ROWS), :]
        return pltpu.make_async_copy(src, dst, sem)

    def start(t, carry):
        row_copy(t, 0).start()
        row_copy(t, 1).start()
        return carry

    lax.fori_loop(0, tm, start, 0, unroll=DMA_UNROLL)
    whole = pltpu.make_async_copy(hp_ref, xs_ref.at[pl.ds(0, tm * PACK_ROWS), :], sem)
    whole.wait()
    whole.wait()

    @pl.when(first)
    def _():
        _zero_fill(seg_ref, xs_ref, zero_ref, zsem, wait=True)


def dispatch(hp, plan, imeta, rows):
    nt, _, tm = imeta.shape
    grid_spec = pltpu.PrefetchScalarGridSpec(
        num_scalar_prefetch=1,
        grid=(nt,),
        in_specs=[pl.BlockSpec((1, 8, tm), lambda i, seg: (i, 0, 0), memory_space=pltpu.SMEM),
                  pl.BlockSpec((tm * PACK_ROWS, LANES), lambda i, seg: (i, 0))],
        out_specs=pl.BlockSpec(memory_space=pl.ANY),
        scratch_shapes=[pltpu.VMEM((ZERO_ROWS * PACK_ROWS, LANES), F32), pltpu.SemaphoreType.DMA(()),
                        pltpu.SemaphoreType.DMA(())],
    )
    return pl.pallas_call(
        _dispatch_body,
        grid_spec=grid_spec,
        out_shape=_sds((rows * PACK_ROWS, LANES), F32),
        compiler_params=_cparams(("arbitrary",), has_side_effects=True),
        name="dispatch",
    )(plan, imeta, hp)


def _expert_body(te_ref, tv_ref, xs_ref, wu_ref, wd_ref, ys_ref, wu_bf, wd_bf):
    i = pl.program_id(0)
    tr = xs_ref.shape[0] // PACK_ROWS

    @pl.when(jnp.logical_or(i == 0, te_ref[i] != te_ref[jnp.maximum(i - 1, 0)]))
    def _():
        wu_bf[...] = wu_ref[0, 0].astype(BF16)
        wd_bf[...] = wd_ref[0, 0].astype(BF16)

    @pl.when(tv_ref[i] == 0)
    def _():
        ys_ref[...] = jnp.zeros_like(ys_ref)

    @pl.when(tv_ref[i] > 0)
    def _():
        x = _load_rows(xs_ref, tr, BF16)
        ac = jnp.dot(x, wu_bf[...], preferred_element_type=F32)
        a = ac[:, :D_EXPERT]
        c = ac[:, D_EXPERT:]
        hmid = (a * jax.nn.sigmoid(a) * c).astype(BF16)
        y = jnp.dot(hmid, wd_bf[...], preferred_element_type=F32)
        _store_rows(ys_ref, y)


def experts(xs, tile_expert, tile_valid, w_up, w_down, layer_idx, tr=ROW_TILE):
    rows = xs.shape[0] // PACK_ROWS
    nt = rows // tr
    grid_spec = pltpu.PrefetchScalarGridSpec(
        num_scalar_prefetch=2,
        grid=(nt,),
        in_specs=[pl.BlockSpec((tr * PACK_ROWS, LANES), lambda i, te, tv: (i, 0)),
                  pl.BlockSpec((1, 1, D_MODEL, 2 * D_EXPERT), lambda i, te, tv: (layer_idx, te[i], 0, 0)),
                  pl.BlockSpec((1, 1, D_EXPERT, D_MODEL), lambda i, te, tv: (layer_idx, te[i], 0, 0))],
        out_specs=pl.BlockSpec((tr * PACK_ROWS, LANES), lambda i, te, tv: (i, 0)),
        scratch_shapes=[pltpu.VMEM((D_MODEL, 2 * D_EXPERT), BF16), pltpu.VMEM((D_EXPERT, D_MODEL), BF16)],
    )
    return pl.pallas_call(
        _expert_body,
        grid_spec=grid_spec,
        out_shape=_sds((rows * PACK_ROWS, LANES), F32),
        compiler_params=_cparams(("arbitrary",)),
        name="experts",
    )(tile_expert, tile_valid, xs, w_up, w_down)


def _combine_body(seg_ref, imeta_ref, imeta_next_ref, ys_ref, x_ref, meta_ref, o_ref, buf, sem):
    i = pl.program_id(0)
    tm = x_ref.shape[0]
    slot = i % 2

    def gather(im_ref, dst_slot):
        def start(t, carry):
            for k in range(2):
                pos = _slot(seg_ref, im_ref, k, t)
                src = ys_ref.at[pl.ds(pl.multiple_of(pos * PACK_ROWS, PACK_ROWS), PACK_ROWS), :]
                dst = buf.at[dst_slot, k, pl.ds(pl.multiple_of(t * PACK_ROWS, PACK_ROWS), PACK_ROWS), :]
                pltpu.make_async_copy(src, dst, sem.at[dst_slot]).start()
            return carry

        lax.fori_loop(0, tm, start, 0, unroll=DMA_UNROLL)

    @pl.when(i == 0)
    def _():
        gather(imeta_ref, 0)

    @pl.when(i + 1 < pl.num_programs(0))
    def _():
        gather(imeta_next_ref, 1 - slot)

    for k in range(2):
        pltpu.make_async_copy(ys_ref.at[pl.ds(0, tm * PACK_ROWS), :], buf.at[slot, k], sem.at[slot]).wait()
    meta = meta_ref[...]
    w1 = meta[:, 2:3]
    w2 = meta[:, 3:4]
    for c in range(PACK_ROWS):
        cols = slice(c * LANES, (c + 1) * LANES)
        o_ref[:, cols] = (x_ref[:, cols] + w1 * buf[slot, 0, pl.ds(c, tm, stride=PACK_ROWS), :]
                          + w2 * buf[slot, 1, pl.ds(c, tm, stride=PACK_ROWS), :])


def combine(ys, seg_start, imeta, x1, meta):
    t, d = x1.shape
    nt, _, tm = imeta.shape
    grid_spec = pltpu.PrefetchScalarGridSpec(
        num_scalar_prefetch=1,
        grid=(nt,),
        in_specs=[pl.BlockSpec((1, 8, tm), lambda i, seg: (i, 0, 0), memory_space=pltpu.SMEM),
                  pl.BlockSpec((1, 8, tm), lambda i, seg: (jnp.minimum(i + 1, nt - 1), 0, 0),
                               memory_space=pltpu.SMEM),
                  pl.BlockSpec(memory_space=pl.ANY),
                  pl.BlockSpec((tm, d), lambda i, seg: (i, 0)),
                  pl.BlockSpec((tm, LANES), lambda i, seg: (i, 0))],
        out_specs=pl.BlockSpec((tm, d), lambda i, seg: (i, 0)),
        scratch_shapes=[pltpu.VMEM((2, 2, tm * PACK_ROWS, LANES), F32), pltpu.SemaphoreType.DMA((2,))],
    )
    return pl.pallas_call(
        _combine_body,
        grid_spec=grid_spec,
        out_shape=_sds((t, d), F32),
        compiler_params=_cparams(("arbitrary",)),
        name="combine",
    )(seg_start, imeta, imeta, ys, x1, meta)


def moe_plan(counts, n_tokens, tr=ROW_TILE):
    n_tiles = (2 * n_tokens) // tr + N_EXPERTS
    cnt = counts[0, :N_EXPERTS].astype(I32)
    tiles = (cnt + tr - 1) // tr
    e = jnp.arange(N_EXPERTS, dtype=I32)
    tile_end = jnp.sum(jnp.where(e[:, None] <= e[None, :], tiles[:, None], 0), axis=0)
    seg_start = (tile_end - tiles) * tr
    n_used = tile_end[N_EXPERTS - 1]
    tid = jnp.arange(n_tiles, dtype=I32)
    tile_valid = (tid < n_used).astype(I32)
    last = jnp.maximum(n_used - 1, 0)
    tile_expert = jnp.sum((tile_end[None, :] <= jnp.minimum(tid, last)[:, None]).astype(I32), axis=1)
    tile_expert = jnp.minimum(tile_expert, N_EXPERTS - 1)
    tail = jnp.stack([n_used * tr, (n_tiles - n_used) * (tr // ZERO_ROWS)])
    plan = jnp.concatenate([seg_start, seg_start + cnt, tiles * tr - cnt, tail]).astype(I32)
    return plan, tile_expert, tile_valid, n_tiles * tr


def mixer_c(z, qn, kn, cos_t, sin_t, batch):
    outs, lses, dils = [], [], []
    for g, (window, dil) in enumerate(DIL_CFG):
        assert window // dil == N_BACK
        o, l = att_c(prep_c(z, g, dil, qn, kn, cos_t, sin_t, batch), dil)
        outs.append(o)
        lses.append(l)
        dils.append(dil)
    return merge_c(outs, lses, dils)


def layer(x, p, lam_init, batch, tabs):
    t, d = x.shape
    s = t // batch
    z, h = proj(x, p["norm_mix"], p["w_in"])
    y_a = mix_a(z, p["sg_norm"], p["sg_w"], p["sg_b"])
    y_b = mix_b(z, p["conv_w"], batch)
    y_c = mixer_c(z, p["qn_c"], p["kn_c"], tabs[0], tabs[1], batch)
    q1, q2, kd, vt = prep_d(z, p["qn_d"], p["kn_d"], tabs[2], tabs[3], s)
    y_d = diff_att(q1.reshape(batch, s, -1), q2.reshape(batch, s, -1), kd.reshape(batch, s, -1), vt,
                   p["lam_q1"], p["lam_k1"], p["lam_q2"], p["lam_k2"], p["subln_d"],
                   lam_init).reshape(t, BRANCH_W)
    merged = gate_merge(h, (y_a, y_b, y_c, y_d), p["w_gate"], p["b_gate"], p["w_branch"])
    x1, hp, meta, counts, imeta = outproj_route(merged, x, p["w_out"], p["norm_ffn"], p["w_r"], p["b_r"])
    seg_start, tile_expert, tile_valid, rows = moe_plan(counts, t)
    xs = dispatch(hp, seg_start, imeta, rows)
    ys = experts(xs, tile_expert, tile_valid, p["w_up"], p["w_down"], p["layer"])
    return combine(ys, seg_start, imeta, x1, meta)


def kernel(x, norm_mix, w_in, sg_norm, sg_w, sg_b, conv_w, qn_c, kn_c, qn_d, kn_d, lam_q1, lam_k1, lam_q2,
           lam_k2, subln_d, w_gate, b_gate, w_branch, w_out, norm_ffn, w_rg, b_rg, w_re, b_re, w_up, w_down):
    batch, seq, d = x.shape
    depth = w_in.shape[0]
    tabs = rope_tables(seq, HEAD_DIM // 2) + rope_tables(seq, DIFF_DIM // 2)
    xf = x.reshape(batch * seq, d)
    for l in range(depth):
        pad = LANES - N_EXPERTS - N_GROUPS
        w_r = jnp.concatenate([w_re[l], w_rg[l], jnp.zeros((d, pad), F32)], axis=1).astype(BF16)
        b_r = jnp.concatenate([b_re[l], b_rg[l], jnp.zeros((pad,), F32)]).reshape(1, LANES)
        p = dict(norm_mix=norm_mix[l], w_in=w_in[l].astype(BF16), sg_norm=sg_norm[l], sg_w=sg_w[l], sg_b=sg_b[l],
                 conv_w=conv_w[l], qn_c=qn_c[l], kn_c=kn_c[l], qn_d=qn_d[l], kn_d=kn_d[l], lam_q1=lam_q1[l],
                 lam_k1=lam_k1[l], lam_q2=lam_q2[l], lam_k2=lam_k2[l], subln_d=subln_d[l],
                 w_gate=w_gate[l].astype(BF16), b_gate=b_gate[l], w_branch=w_branch[l].astype(BF16),
                 w_out=w_out[l].astype(BF16), norm_ffn=norm_ffn[l], w_r=w_r, b_r=b_r,
                 w_up=w_up, w_down=w_down, layer=l)
        xf = layer(xf, p, 0.8 - 0.6 * math.exp(-0.3 * l), batch, tabs)
    return xf.reshape(batch, seq, d)
```
